```python
import jax
import jax.numpy as jnp
from jax import lax
import numpy as np

D_MODEL = 2048
BATCH = 2
SEQ = 4096
DEPTH = 4

N_MIXERS = 3
N_RWKV = (DEPTH + 2) // 3
N_RET = (DEPTH + 1) // 3
N_HGRN = DEPTH // 3

RW_HEAD = 64
RW_HEADS = D_MODEL // RW_HEAD
RW_DECAY_LORA = max(32, int(round(D_MODEL ** 0.5 * 1.8 / 32)) * 32)
RW_AAA_LORA = max(32, int(round(D_MODEL ** 0.5 * 1.8 / 32)) * 32)
RW_MV_LORA = max(32, int(round(D_MODEL ** 0.5 * 1.3 / 32)) * 32)
RW_GATE_LORA = max(32, int(round(D_MODEL ** 0.8 * 0.6 / 32)) * 32)
RW_GN_EPS = RW_HEAD * 1e-5

RET_QK_HEAD = 256
RET_HEADS = D_MODEL // RET_QK_HEAD
RET_V_HEAD = 2 * RET_QK_HEAD
RET_QK_DIM = RET_HEADS * RET_QK_HEAD
RET_V_DIM = RET_HEADS * RET_V_HEAD
RET_PROJ = 2 * RET_QK_DIM + 2 * RET_V_DIM
RET_CHUNK = 128
ROPE_BASE = 10000.0
GN_EPS = 1e-5

HG_EXPAND = 128
HG_HEADS = D_MODEL // HG_EXPAND
HG_DIM = HG_HEADS * HG_EXPAND
HG_CHUNK = 64

D_FF = int(round(8 * D_MODEL / 3 / 128)) * 128
FFN_CONV = 3
RMS_EPS = 1e-6

kernel_name = "hybrid_rwkv7_retnet_hgrn2_trunk"


def rms_norm(x, w, eps=RMS_EPS):
    xf = x.astype(jnp.float32)
    y = xf * lax.rsqrt(jnp.mean(xf * xf, axis=-1, keepdims=True) + eps)
    return (y * w.astype(jnp.float32)).astype(x.dtype)


def head_group_norm(y, eps):
    yf = y.astype(jnp.float32)
    mu = jnp.mean(yf, axis=-1, keepdims=True)
    var = jnp.mean(jnp.square(yf - mu), axis=-1, keepdims=True)
    return (yf - mu) * lax.rsqrt(var + eps)


def token_shift(x):
    return jnp.pad(x, ((0, 0), (1, 0), (0, 0)))[:, :-1]


def causal_dwconv(h, w):
    k = w.shape[0]
    s = h.shape[1]
    hp = jnp.pad(h, ((0, 0), (k - 1, 0), (0, 0)))
    out = hp[:, 0:s] * w[0]
    for j in range(1, k):
        out = out + hp[:, j:j + s] * w[j]
    return out


def to_chunks(t, c):
    b, s, h, d = t.shape
    return t.reshape(b, s // c, c, h, d).transpose(1, 0, 3, 2, 4)


def from_chunks(t):
    n, b, h, c, d = t.shape
    return t.transpose(1, 0, 3, 2, 4).reshape(b, n * c, h, d)


def conv_glu_ffn(x, w_up, w_conv, w_down):
    u = causal_dwconv(x @ w_up, w_conv)
    gate, val = jnp.split(u, 2, axis=-1)
    return (jax.nn.silu(gate) * val) @ w_down


def rwkv7_scan(r, decay, k, v, kk, a):
    b, s, h, n = r.shape

    def step(state, inp):
        r_t, w_t, k_t, v_t, ma_t, b_t = inp
        sa = jnp.einsum("bhij,bhj->bhi", state, ma_t)
        state = (state * w_t[:, :, None, :]
                 + sa[..., None] * b_t[:, :, None, :]
                 + v_t[..., None] * k_t[:, :, None, :])
        return state, jnp.einsum("bhij,bhj->bhi", state, r_t)

    seq = tuple(jnp.moveaxis(t.astype(jnp.float32), 1, 0) for t in (r, decay, k, v, -kk, kk * a))
    state0 = jnp.zeros((b, h, n, n), jnp.float32)
    _, y = lax.scan(step, state0, seq)
    return jnp.moveaxis(y, 0, 1)


def rwkv7_time_mix(x, v_first, vres, mu, w_rkv, w0, w1, w2, a0, a1, a2, g1, g2,
                   k_k, k_a, r_k, lnx_w, lnx_b, w_o):
    b, s, d = x.shape
    xx = token_shift(x) - x
    xm = x[None] + xx[None] * mu[:, None, None, :]
    rkv = jnp.einsum("jbsd,jde->jbse", xm[:3], w_rkv)
    r, k, v = rkv[0], rkv[1], rkv[2]
    xw, xa, xg = xm[3], xm[4], xm[5]
    w = -jax.nn.softplus(-(w0 + jnp.tanh(xw @ w1) @ w2)) - 0.5
    decay = jnp.exp(-jnp.exp(w.astype(jnp.float32)))
    a = jax.nn.sigmoid(a0 + (xa @ a1) @ a2)
    g = jax.nn.sigmoid(xg @ g1) @ g2
    if vres is None:
        v_first = v
    else:
        v0, v1, v2 = vres
        v = v + (v_first - v) * jax.nn.sigmoid(v0 + (xm[2] @ v1) @ v2)

    def heads(t):
        return t.reshape(b, s, RW_HEADS, RW_HEAD)

    kk = heads(k * k_k).astype(jnp.float32)
    kk = kk / jnp.maximum(jnp.sqrt(jnp.sum(kk * kk, axis=-1, keepdims=True)), 1e-12)
    k = k * (1.0 + (a - 1.0) * k_a)
    r_h, k_h, v_h = heads(r), heads(k), heads(v)
    y = rwkv7_scan(r_h, heads(decay), k_h, v_h, kk, heads(a))
    y = head_group_norm(y, RW_GN_EPS).reshape(b, s, d) * lnx_w + lnx_b
    bonus = jnp.sum(r_h * k_h * r_k, axis=-1, keepdims=True) * v_h
    out = ((y.astype(x.dtype) + bonus.reshape(b, s, d)) * g) @ w_o
    return out, v_first


def rotary(t, pos):
    d = t.shape[-1]
    inv_freq = 1.0 / (ROPE_BASE ** (jnp.arange(0, d, 2, dtype=jnp.float32) / d))
    ang = pos[:, None] * inv_freq[None, :]
    cos = jnp.cos(ang)[None, :, None, :]
    sin = jnp.sin(ang)[None, :, None, :]
    t1, t2 = t[..., : d // 2], t[..., d // 2:]
    return jnp.concatenate([t1 * cos - t2 * sin, t1 * sin + t2 * cos], axis=-1)


def retention_chunkwise(q, k, v):
    b, s, h, dk = q.shape
    c = RET_CHUNK
    log_gamma = jnp.log1p(-jnp.exp2(-5.0 - jnp.arange(h, dtype=jnp.float32)))
    pos = jnp.arange(c, dtype=jnp.float32)
    rel = pos[:, None] - pos[None, :]
    intra = jnp.where(rel >= 0, jnp.exp(log_gamma[:, None, None] * jnp.maximum(rel, 0.0)), 0.0)
    cross = jnp.exp(log_gamma[:, None] * (pos + 1.0))[None, :, :, None]
    tail = jnp.exp(log_gamma[:, None] * (c - 1.0 - pos))[None, :, :, None]
    chunk_decay = jnp.exp(log_gamma * c)[None, :, None, None]

    def step(state, inp):
        qc, kc, vc = inp
        scores = jnp.einsum("bhnd,bhmd->bhnm", qc, kc) * intra
        o = (jnp.einsum("bhnm,bhme->bhne", scores, vc)
             + jnp.einsum("bhnd,bhde->bhne", qc, state) * cross)
        state = state * chunk_decay + jnp.einsum("bhmd,bhme->bhde", kc * tail, vc)
        return state, o

    seq = tuple(to_chunks(t.astype(jnp.float32), c) for t in (q, k, v))
    state0 = jnp.zeros((b, h, dk, v.shape[-1]), jnp.float32)
    _, o = lax.scan(step, state0, seq)
    return from_chunks(o)


def retention_mix(x, w_in, gn_w, w_o):
    b, s, _ = x.shape
    q, k, v, g = jnp.split(x @ w_in, [RET_QK_DIM, 2 * RET_QK_DIM, 2 * RET_QK_DIM + RET_V_DIM], axis=-1)
    pos = jnp.arange(s, dtype=jnp.float32)
    q = rotary(q.reshape(b, s, RET_HEADS, RET_QK_HEAD).astype(jnp.float32), pos)
    k = rotary(k.reshape(b, s, RET_HEADS, RET_QK_HEAD).astype(jnp.float32), pos) * (RET_QK_HEAD ** -0.5)
    v = v.reshape(b, s, RET_HEADS, RET_V_HEAD)
    o = retention_chunkwise(q, k, v)
    o = head_group_norm(o, GN_EPS).reshape(b, s, RET_V_DIM) * gn_w
    return (jax.nn.silu(g) * o.astype(x.dtype)) @ w_o


def gla_chunkwise(q, k, v, log_f):
    b, s, h, dk = q.shape
    c = HG_CHUNK
    causal = jnp.tril(jnp.ones((c, c), dtype=bool))[None, None, :, :, None]

    def step(state, inp):
        qc, kc, vc, lc = inp
        cum = jnp.cumsum(lc, axis=2)
        pair = jnp.exp(jnp.where(causal, cum[:, :, :, None, :] - cum[:, :, None, :, :], -jnp.inf))
        scores = jnp.einsum("bhnd,bhmd,bhnmd->bhnm", qc, kc, pair)
        o = (jnp.einsum("bhnm,bhme->bhne", scores, vc)
             + jnp.einsum("bhnd,bhde->bhne", qc * jnp.exp(cum), state))
        last = cum[:, :, -1:, :]
        state = (jnp.exp(last[:, :, 0, :])[..., None] * state
                 + jnp.einsum("bhmd,bhme->bhde", kc * jnp.exp(last - cum), vc))
        return state, o

    seq = tuple(to_chunks(t, c) for t in (q, k, v, log_f))
    state0 = jnp.zeros((b, h, dk, v.shape[-1]), jnp.float32)
    _, o = lax.scan(step, state0, seq)
    return from_chunks(o)


def hgrn2_mix(x, lb, w_in, norm_w, w_o):
    b, s, _ = x.shape
    q, f_raw, i, g = jnp.split(x @ w_in, 4, axis=-1)

    def heads(t):
        return t.astype(jnp.float32).reshape(b, s, HG_HEADS, HG_EXPAND)

    f_raw = f_raw.astype(jnp.float32)
    log_f = jnp.logaddexp(jnp.log(lb), jnp.log1p(-lb) + jax.nn.log_sigmoid(f_raw))
    k = (1.0 - lb) * jax.nn.sigmoid(-f_raw)
    o = gla_chunkwise(heads(jax.nn.silu(q)), heads(k), heads(i), heads(log_f))
    o = o * lax.rsqrt(jnp.mean(o * o, axis=-1, keepdims=True) + RMS_EPS)
    o = o.reshape(b, s, HG_DIM) * norm_w
    return (o.astype(x.dtype) * jax.nn.sigmoid(g)) @ w_o


def setup_inputs(seed: int = 0) -> dict:
    key = jax.random.key(seed)
    ks = iter(jax.random.split(key, 40))
    f32 = jnp.float32

    def nrm(shape, scale):
        return scale * jax.random.normal(next(ks), shape, f32)

    d = D_MODEL
    n_a, n_b, n_c = N_RWKV, N_RET, N_HGRN
    n_v = max(n_a - 1, 0)
    decay_base = -6.5 + 5.0 * (jnp.arange(d, dtype=f32) / (d - 1)) ** 0.85
    return {
        "x": nrm((BATCH, SEQ, d), 1.0),
        "norm_mix": 1.0 + nrm((DEPTH, d), 0.02),
        "norm_ffn": 1.0 + nrm((DEPTH, d), 0.02),
        "norm_final": 1.0 + nrm((d,), 0.02),
        "ffn_w_up": nrm((DEPTH, d, 2 * D_FF), d ** -0.5),
        "ffn_conv": nrm((DEPTH, FFN_CONV, 2 * D_FF), FFN_CONV ** -0.5),
        "ffn_w_down": nrm((DEPTH, D_FF, d), D_FF ** -0.5),
        "rw_mu": jax.random.uniform(next(ks), (n_a, 6, d), f32),
        "rw_w_rkv": nrm((n_a, 3, d, d), d ** -0.5),
        "rw_w0": decay_base[None, :] + nrm((n_a, d), 0.1),
        "rw_w1": nrm((n_a, d, RW_DECAY_LORA), d ** -0.5),
        "rw_w2": nrm((n_a, RW_DECAY_LORA, d), 0.1 * RW_DECAY_LORA ** -0.5),
        "rw_a0": nrm((n_a, d), 0.1),
        "rw_a1": nrm((n_a, d, RW_AAA_LORA), d ** -0.5),
        "rw_a2": nrm((n_a, RW_AAA_LORA, d), 0.1 * RW_AAA_LORA ** -0.5),
        "rw_v0": 1.0 + nrm((n_v, d), 0.1),
        "rw_v1": nrm((n_v, d, RW_MV_LORA), d ** -0.5),
        "rw_v2": nrm((n_v, RW_MV_LORA, d), 0.1 * RW_MV_LORA ** -0.5),
        "rw_g1": nrm((n_a, d, RW_GATE_LORA), d ** -0.5),
        "rw_g2": nrm((n_a, RW_GATE_LORA, d), RW_GATE_LORA ** -0.5),
        "rw_k_k": 0.85 + nrm((n_a, d), 0.02),
        "rw_k_a": 1.0 + nrm((n_a, d), 0.02),
        "rw_r_k": nrm((n_a, RW_HEADS, RW_HEAD), 0.1),
        "rw_lnx_w": 1.0 + nrm((n_a, d), 0.02),
        "rw_lnx_b": nrm((n_a, d), 0.02),
        "rw_w_o": nrm((n_a, d, d), d ** -0.5),
        "ret_w_in": nrm((n_b, d, RET_PROJ), d ** -0.5),
        "ret_gn_w": 1.0 + nrm((n_b, RET_V_DIM), 0.02),
        "ret_w_o": nrm((n_b, RET_V_DIM, d), RET_V_DIM ** -0.5),
        "hg_w_in": nrm((n_c, d, 4 * HG_DIM), d ** -0.5),
        "hg_lb_logits": nrm((DEPTH, HG_DIM), 0.5),
        "hg_norm_w": 1.0 + nrm((n_c, HG_DIM), 0.02),
        "hg_w_o": nrm((n_c, HG_DIM, d), HG_DIM ** -0.5),
    }


def reference(x, norm_mix, norm_ffn, norm_final, ffn_w_up, ffn_conv, ffn_w_down,
              rw_mu, rw_w_rkv, rw_w0, rw_w1, rw_w2, rw_a0, rw_a1, rw_a2,
              rw_v0, rw_v1, rw_v2, rw_g1, rw_g2, rw_k_k, rw_k_a, rw_r_k,
              rw_lnx_w, rw_lnx_b, rw_w_o, ret_w_in, ret_gn_w, ret_w_o,
              hg_w_in, hg_lb_logits, hg_norm_w, hg_w_o):
    lb_all = jnp.cumsum(jax.nn.softmax(hg_lb_logits.astype(jnp.float32), axis=0), axis=0)
    lb_all = lb_all - lb_all[0]
    h = x
    v_first = None
    for layer in range(DEPTH):
        kind = layer % N_MIXERS
        j = layer // N_MIXERS
        xn = rms_norm(h, norm_mix[layer])
        if kind == 0:
            vres = None if j == 0 else (rw_v0[j - 1], rw_v1[j - 1], rw_v2[j - 1])
            mix, v_first = rwkv7_time_mix(
                xn, v_first, vres, rw_mu[j], rw_w_rkv[j], rw_w0[j], rw_w1[j], rw_w2[j],
                rw_a0[j], rw_a1[j], rw_a2[j], rw_g1[j], rw_g2[j], rw_k_k[j], rw_k_a[j],
                rw_r_k[j], rw_lnx_w[j], rw_lnx_b[j], rw_w_o[j])
        elif kind == 1:
            mix = retention_mix(xn, ret_w_in[j], ret_gn_w[j], ret_w_o[j])
        else:
            mix = hgrn2_mix(xn, lb_all[layer], hg_w_in[j], hg_norm_w[j], hg_w_o[j])
        h = h + mix.astype(h.dtype)
        h = h + conv_glu_ffn(rms_norm(h, norm_ffn[layer]), ffn_w_up[layer],
                             ffn_conv[layer], ffn_w_down[layer]).astype(h.dtype)
    return rms_norm(h, norm_final)
```

```python
import functools
import math

import jax
import jax.numpy as jnp
from jax import lax
from jax.experimental import pallas as pl
from jax.experimental.pallas import tpu as pltpu

F32 = jnp.float32
BF16 = jnp.bfloat16

RMS_EPS = 1e-6
GN_EPS = 1e-5
ROPE_BASE = 10000.0

RW_HEAD = 64
RW_GN_EPS = RW_HEAD * 1e-5
RW_CHUNK = 64
RW_GROUP = 4

RET_QK_HEAD = 256
RET_V_HEAD = 512
RET_CHUNK = 128

HG_HEAD = 128
HG_SUB = 16
HG_CHUNK = 64

FFN_CONV = 3
LANE = 128
ROW_TILE = 512
FFN_TN = 512

VMEM_LIMIT = 56 * 1024 * 1024


def _dot(a, b):
    return jnp.dot(a, b, preferred_element_type=F32)


def _dot_nt(a, b):
    return lax.dot_general(a, b, (((1,), (1,)), ((), ())), preferred_element_type=F32)


def _dot_tn(a, b):
    return lax.dot_general(a, b, (((0,), (0,)), ((), ())), preferred_element_type=F32)


def _params(*sem):
    return pltpu.CompilerParams(dimension_semantics=sem, vmem_limit_bytes=VMEM_LIMIT)


def _rms(x, w):
    ms = jnp.mean(x * x, axis=-1, keepdims=True)
    return x * lax.rsqrt(ms + RMS_EPS) * w


def _rmsnorm_kernel(x_ref, w_ref, o_ref):
    o_ref[...] = _rms(x_ref[...], w_ref[...]).astype(o_ref.dtype)


def rmsnorm(x, w, out_dtype=F32):
    m, d = x.shape
    tm = min(ROW_TILE, m)
    return pl.pallas_call(
        _rmsnorm_kernel,
        out_shape=jax.ShapeDtypeStruct((m, d), out_dtype),
        grid=(m // tm,),
        in_specs=[pl.BlockSpec((tm, d), lambda i: (i, 0)),
                  pl.BlockSpec((1, d), lambda i: (0, 0))],
        out_specs=pl.BlockSpec((tm, d), lambda i: (i, 0)),
        compiler_params=_params("parallel"),
        name="rmsnorm",
    )(x, w.reshape(1, d))


def _mm_kernel(x_ref, w_ref, o_ref):
    o_ref[...] = _dot(x_ref[...], w_ref[...]).astype(o_ref.dtype)


def _mm_res_kernel(x_ref, w_ref, r_ref, o_ref):
    o_ref[...] = (r_ref[...] + _dot(x_ref[...], w_ref[...])).astype(o_ref.dtype)


def _col_tile(n):
    for t in (1024, 512, 256, 128):
        if n % t == 0:
            return t
    return n


def matmul(x, w, res=None, out_dtype=F32):
    m, k = x.shape
    n = w.shape[1]
    tm = min(ROW_TILE, m)
    tn = _col_tile(n)
    in_specs = [pl.BlockSpec((tm, k), lambda i, j: (i, 0)),
                pl.BlockSpec((k, tn), lambda i, j: (0, j))]
    args = [x, w]
    body = _mm_kernel
    if res is not None:
        in_specs.append(pl.BlockSpec((tm, tn), lambda i, j: (i, j)))
        args.append(res)
        body = _mm_res_kernel
    return pl.pallas_call(
        body,
        out_shape=jax.ShapeDtypeStruct((m, n), out_dtype),
        grid=(m // tm, n // tn),
        in_specs=in_specs,
        out_specs=pl.BlockSpec((tm, tn), lambda i, j: (i, j)),
        compiler_params=_params("parallel", "parallel"),
        name="matmul_res" if res is not None else "matmul",
    )(*args)


def _norm_mm_kernel(h_ref, nw_ref, w_ref, o_ref, xs_ref):
    @pl.when(pl.program_id(1) == 0)
    def _():
        xs_ref[...] = _rms(h_ref[...], nw_ref[...]).astype(BF16)

    o_ref[...] = _dot(xs_ref[...], w_ref[...]).astype(o_ref.dtype)


def norm_matmul(h, nw, w, out_dtype=F32):
    m, d = h.shape
    n = w.shape[1]
    tm = min(ROW_TILE, m)
    tn = _col_tile(n)
    return pl.pallas_call(
        _norm_mm_kernel,
        out_shape=jax.ShapeDtypeStruct((m, n), out_dtype),
        grid=(m // tm, n // tn),
        in_specs=[pl.BlockSpec((tm, d), lambda i, j: (i, 0)),
                  pl.BlockSpec((1, d), lambda i, j: (0, 0)),
                  pl.BlockSpec((d, tn), lambda i, j: (0, j))],
        out_specs=pl.BlockSpec((tm, tn), lambda i, j: (i, j)),
        scratch_shapes=[pltpu.VMEM((tm, d), BF16)],
        compiler_params=_params("parallel", "arbitrary"),
        name="norm_matmul",
    )(h, nw.reshape(1, d), w)


def _shift_rows(u, carry, fresh):
    tm = u.shape[0]
    c = jnp.where(fresh, 0.0, carry)
    row = lax.broadcasted_iota(jnp.int32, u.shape, 0)
    u1 = jnp.where(row == 0, c[7:8, :], pltpu.roll(u, 1, axis=0))
    u2 = jnp.where(row == 0, c[6:7, :], jnp.where(row == 1, c[7:8, :], pltpu.roll(u, 2, axis=0)))
    del tm
    return u1, u2


def _ffn_kernel(h_ref, nw_ref, wg_ref, wv_ref, cg_ref, cv_ref, wd_ref, o_ref,
                xs_ref, carry_g_ref, carry_v_ref, *, tiles_per_seq):
    i = pl.program_id(0)
    j = pl.program_id(1)

    @pl.when(j == 0)
    def _():
        h = h_ref[...]
        xs_ref[...] = _rms(h, nw_ref[...]).astype(BF16)
        o_ref[...] = h

    fresh = (i % tiles_per_seq) == 0
    xs = xs_ref[...]

    def conv(w_ref, c_ref, carry_ref):
        u = _dot(xs, w_ref[...])
        u1, u2 = _shift_rows(u, carry_ref[j], fresh)
        carry_ref[j] = u[u.shape[0] - 8:, :]
        cw = c_ref[...]
        return u2 * cw[0:1, :] + u1 * cw[1:2, :] + u * cw[2:3, :]

    gate = conv(wg_ref, cg_ref, carry_g_ref)
    val = conv(wv_ref, cv_ref, carry_v_ref)
    act = (gate * jax.nn.sigmoid(gate) * val).astype(BF16)
    o_ref[...] += _dot(act, wd_ref[...])


def ffn_block(h, nw, w_up, w_conv, w_down, seq):
    m, d = h.shape
    f = w_down.shape[0]
    tm = min(ROW_TILE, seq)
    tn = FFN_TN
    nt = f // tn
    kern = functools.partial(_ffn_kernel, tiles_per_seq=seq // tm)
    return pl.pallas_call(
        kern,
        out_shape=jax.ShapeDtypeStruct((m, d), F32),
        grid=(m // tm, nt),
        in_specs=[pl.BlockSpec((tm, d), lambda i, j: (i, 0)),
                  pl.BlockSpec((1, d), lambda i, j: (0, 0)),
                  pl.BlockSpec((d, tn), lambda i, j: (0, j)),
                  pl.BlockSpec((d, tn), lambda i, j: (0, j + nt)),
                  pl.BlockSpec((FFN_CONV, tn), lambda i, j: (0, j)),
                  pl.BlockSpec((FFN_CONV, tn), lambda i, j: (0, j + nt)),
                  pl.BlockSpec((tn, d), lambda i, j: (j, 0))],
        out_specs=pl.BlockSpec((tm, d), lambda i, j: (i, 0)),
        scratch_shapes=[pltpu.VMEM((tm, d), BF16),
                        pltpu.VMEM((nt, 8, tn), F32),
                        pltpu.VMEM((nt, 8, tn), F32)],
        compiler_params=_params("arbitrary", "arbitrary"),
        name="ffn",
    )(h, nw.reshape(1, d), w_up, w_up, w_conv, w_conv, w_down)


def _prep_ffn(w_up, w_conv, w_down):
    f = w_down.shape[0]
    fp = -(-f // FFN_TN) * FFN_TN
    pad = fp - f
    up = jnp.concatenate([jnp.pad(w_up[:, :f], ((0, 0), (0, pad))),
                          jnp.pad(w_up[:, f:], ((0, 0), (0, pad)))], axis=1).astype(BF16)
    cv = jnp.concatenate([jnp.pad(w_conv[:, :f], ((0, 0), (0, pad))),
                          jnp.pad(w_conv[:, f:], ((0, 0), (0, pad)))], axis=1)
    down = jnp.pad(w_down, ((0, pad), (0, 0))).astype(BF16)
    return up, cv, down


def _rwkv_chunk_kernel(r_ref, ld_ref, k_ref, v_ref, kk_ref, a_ref, y_ref, z_ref):
    c = pl.program_id(2)

    @pl.when(c == 0)
    def _():
        z_ref[...] = jnp.zeros_like(z_ref)

    C, W = r_ref.shape
    G = W // RW_HEAD
    r = r_ref[...]
    ld = ld_ref[...]
    k = k_ref[...]
    v = v_ref[...]
    kk = kk_ref[...]
    a = a_ref[...]

    trow = lax.broadcasted_iota(jnp.int32, (C, C), 0)
    tcol = lax.broadcasted_iota(jnp.int32, (C, C), 1)
    tri = (trow >= tcol).astype(BF16)
    ld_hi = ld.astype(BF16)
    ld_lo = (ld - ld_hi.astype(F32)).astype(BF16)
    cum = _dot(tri, ld_hi) + _dot(tri, ld_lo)
    last = cum[C - 1:C, :]

    b = kk * a
    einv = jnp.exp(-cum)
    etail = jnp.exp(last - cum)
    at = -kk * jnp.exp(cum - ld)
    rt = r * jnp.exp(cum)
    bt = b * einv
    kt = k * einv
    bh = b * etail
    kh = k * etail
    pc = jnp.exp(last)

    brow = lax.broadcasted_iota(jnp.int32, (W, W), 0)
    bcol = lax.broadcasted_iota(jnp.int32, (W, W), 1)
    bdm = (brow // C) == (bcol // RW_HEAD)

    def bd(x):
        return jnp.where(bdm, jnp.concatenate([x.astype(BF16)] * G, axis=0), 0)

    lrow = lax.broadcasted_iota(jnp.int32, (C, W), 0)
    lcol = lax.broadcasted_iota(jnp.int32, (C, W), 1) % RW_HEAD
    strict = lrow > lcol
    incl = lrow >= lcol

    vb = bd(v)
    lhs = jnp.concatenate([at, rt], axis=0).astype(BF16)
    rhs = jnp.concatenate([bd(bt), bd(kt)], axis=0)
    gram = _dot_nt(lhs, rhs)
    a_ab = jnp.where(strict, gram[:C, :W], 0.0)
    a_ak = jnp.where(strict, gram[:C, W:], 0.0)
    a_rb = jnp.where(incl, gram[C:, :W], 0.0).astype(BF16)
    a_rk = jnp.where(incl, gram[C:, W:], 0.0).astype(BF16)

    p = a_ab
    t = jnp.where(lrow == lcol, 1.0, 0.0) + a_ab
    pb = bd(p)
    steps = int(math.log2(C)) - 1
    for _ in range(steps):
        p = _dot(p.astype(BF16), pb)
        pb = bd(p)
        t = t + _dot(t.astype(BF16), pb)
    tb = t.astype(BF16)

    wa = _dot(tb, bd(at))
    x1 = _dot(a_ak.astype(BF16), vb)
    uv = _dot(tb, bd(x1))
    qp = rt + _dot(a_rb, bd(wa))
    yv = _dot(a_rb, bd(uv)) + _dot(a_rk, vb)

    z = z_ref[...]
    zb = z.astype(BF16)
    y_ref[...] = _dot_nt(qp.astype(BF16), zb) + yv

    mc = jnp.where(bdm, _dot_tn(wa.astype(BF16), bh.astype(BF16)), 0.0)
    gm = jnp.where(bdm, _dot_tn(jnp.concatenate([uv, v], axis=0).astype(BF16),
                                jnp.concatenate([bh, kh], axis=0).astype(BF16)), 0.0)
    z_ref[...] = z * pc + _dot(zb, mc.astype(BF16)) + gm


def rwkv_scan(r, ld, k, v, kk, a, batch, seq):
    m, d = r.shape
    C = RW_CHUNK
    W = RW_GROUP * RW_HEAD
    assert RW_GROUP * C == W
    nc = seq // C
    spec = pl.BlockSpec((C, W), lambda b, g, c: (b * nc + c, g))
    return pl.pallas_call(
        _rwkv_chunk_kernel,
        out_shape=jax.ShapeDtypeStruct((m, d), F32),
        grid=(batch, d // W, nc),
        in_specs=[spec] * 6,
        out_specs=spec,
        scratch_shapes=[pltpu.VMEM((W, W), F32)],
        compiler_params=_params("parallel", "parallel", "arbitrary"),
        name="rwkv_scan",
    )(r, ld, k, v, kk, a)


def _retention_kernel(q_ref, k_ref, v_ref, g_ref, cos_ref, sin_ref, intra_ref, cross_ref,
                      tail_ref, gnw_ref, o_ref, st_ref):
    c = pl.program_id(2)

    @pl.when(c == 0)
    def _():
        st_ref[...] = jnp.zeros_like(st_ref)

    C = q_ref.shape[0]
    half = RET_QK_HEAD // 2
    cos = cos_ref[...]
    sin = sin_ref[...]

    def rope(t):
        t1 = t[:, :half]
        t2 = t[:, half:]
        return jnp.concatenate([t1 * cos - t2 * sin, t1 * sin + t2 * cos], axis=-1)

    q = rope(q_ref[...])
    k = rope(k_ref[...]) * (RET_QK_HEAD ** -0.5)
    vb = v_ref[...].astype(BF16)
    qb = q.astype(BF16)
    cross = cross_ref[...]
    scores = _dot_nt(qb, k.astype(BF16)) * intra_ref[...]
    st = st_ref[...]
    o = _dot(scores.astype(BF16), vb) + _dot(qb, st.astype(BF16)) * cross
    st_ref[...] = st * cross[C - 1:C, :] + _dot_tn((k * tail_ref[...]).astype(BF16), vb)

    mu = jnp.mean(o, axis=-1, keepdims=True)
    oc = o - mu
    var = jnp.mean(oc * oc, axis=-1, keepdims=True)
    on = oc * lax.rsqrt(var + GN_EPS) * gnw_ref[...]
    g = g_ref[...]
    o_ref[...] = (g * jax.nn.sigmoid(g) * on).astype(o_ref.dtype)


def retention(proj, gn_w, batch, seq):
    m, n = proj.shape
    heads = n // (2 * RET_QK_HEAD + 2 * RET_V_HEAD)
    C = RET_CHUNK
    nc = seq // C
    half = RET_QK_HEAD // 2
    pos = jnp.arange(seq, dtype=F32)
    inv_freq = 1.0 / (ROPE_BASE ** (jnp.arange(0, RET_QK_HEAD, 2, dtype=F32) / RET_QK_HEAD))
    ang = pos[:, None] * inv_freq[None, :]
    cos, sin = jnp.cos(ang), jnp.sin(ang)
    log_gamma = jnp.log1p(-jnp.exp2(-5.0 - jnp.arange(heads, dtype=F32)))
    cp = jnp.arange(C, dtype=F32)
    rel = cp[:, None] - cp[None, :]
    intra = jnp.where(rel >= 0, jnp.exp(log_gamma[:, None, None] * jnp.maximum(rel, 0.0)), 0.0)
    cross = jnp.exp(log_gamma[:, None] * (cp + 1.0))[:, :, None]
    tail = jnp.exp(log_gamma[:, None] * (C - 1.0 - cp))[:, :, None]
    vdim = heads * RET_V_HEAD
    voff = 2 * heads * RET_QK_HEAD // RET_V_HEAD
    return pl.pallas_call(
        _retention_kernel,
        out_shape=jax.ShapeDtypeStruct((m, vdim), BF16),
        grid=(batch, heads, nc),
        in_specs=[pl.BlockSpec((C, RET_QK_HEAD), lambda b, h, c: (b * nc + c, h)),
                  pl.BlockSpec((C, RET_QK_HEAD), lambda b, h, c: (b * nc + c, heads + h)),
                  pl.BlockSpec((C, RET_V_HEAD), lambda b, h, c: (b * nc + c, voff + h)),
                  pl.BlockSpec((C, RET_V_HEAD), lambda b, h, c: (b * nc + c, voff + heads + h)),
                  pl.BlockSpec((C, half), lambda b, h, c: (c, 0)),
                  pl.BlockSpec((C, half), lambda b, h, c: (c, 0)),
                  pl.BlockSpec((None, C, C), lambda b, h, c: (h, 0, 0)),
                  pl.BlockSpec((None, C, 1), lambda b, h, c: (h, 0, 0)),
                  pl.BlockSpec((None, C, 1), lambda b, h, c: (h, 0, 0)),
                  pl.BlockSpec((1, RET_V_HEAD), lambda b, h, c: (0, h))],
        out_specs=pl.BlockSpec((C, RET_V_HEAD), lambda b, h, c: (b * nc + c, h)),
        scratch_shapes=[pltpu.VMEM((RET_QK_HEAD, RET_V_HEAD), F32)],
        compiler_params=_params("parallel", "parallel", "arbitrary"),
        name="retention",
    )(proj, proj, proj, proj, cos, sin, intra, cross, tail, gn_w.reshape(1, vdim))


def _hgrn_kernel(q_ref, f_ref, i_ref, g_ref, lb_ref, nw_ref, o_ref, zt_ref):
    c = pl.program_id(2)

    @pl.when(c == 0)
    def _():
        zt_ref[...] = jnp.zeros_like(zt_ref)

    C, D = q_ref.shape
    B = HG_SUB
    lb = lb_ref[...]
    nw = nw_ref[...]
    trow = lax.broadcasted_iota(jnp.int32, (B, B), 0)
    tcol = lax.broadcasted_iota(jnp.int32, (B, B), 1)
    tri = (trow >= tcol).astype(BF16)
    n_idx = lax.broadcasted_iota(jnp.int32, (B, B, D), 0)
    m_idx = lax.broadcasted_iota(jnp.int32, (B, B, D), 1)
    causal = n_idx >= m_idx
    ones = jnp.ones((D, D), BF16)

    zt = zt_ref[...]
    for s in range(C // B):
        rows = slice(s * B, (s + 1) * B)
        qr = q_ref[rows, :]
        fr = f_ref[rows, :]
        v = i_ref[rows, :]
        g = g_ref[rows, :]
        q = qr * jax.nn.sigmoid(qr)
        sig = jax.nn.sigmoid(fr)
        lf = jnp.log(lb + (1.0 - lb) * sig)
        k = (1.0 - lb) * (1.0 - sig)
        lf_hi = lf.astype(BF16)
        lf_lo = (lf - lf_hi.astype(F32)).astype(BF16)
        cum = _dot(tri, lf_hi) + _dot(tri, lf_lo)
        last = cum[B - 1:B, :]
        o = _dot_nt((q * jnp.exp(cum)).astype(BF16), zt.astype(BF16))
        e = jnp.exp(jnp.where(causal, cum[:, None, :] - cum[None, :, :], -jnp.inf))
        p = (q[:, None, :] * k[None, :, :] * e).reshape(B * B, D)
        srep = _dot(p.astype(BF16), ones).reshape(B, B, D)
        o = o + jnp.sum(srep * v[None, :, :], axis=1)
        zt = zt * jnp.exp(last) + _dot_tn(v.astype(BF16), (k * jnp.exp(last - cum)).astype(BF16))
        on = o * lax.rsqrt(jnp.mean(o * o, axis=-1, keepdims=True) + RMS_EPS) * nw
        o_ref[rows, :] = (on * jax.nn.sigmoid(g)).astype(o_ref.dtype)
    zt_ref[...] = zt


def hgrn_mix(proj, lb, norm_w, batch, seq):
    m, n = proj.shape
    d = n // 4
    heads = d // HG_HEAD
    C = HG_CHUNK
    nc = seq // C

    def spec(off):
        return pl.BlockSpec((C, HG_HEAD), lambda b, h, c: (b * nc + c, off * heads + h))

    vec = pl.BlockSpec((1, HG_HEAD), lambda b, h, c: (0, h))
    return pl.pallas_call(
        _hgrn_kernel,
        out_shape=jax.ShapeDtypeStruct((m, d), BF16),
        grid=(batch, heads, nc),
        in_specs=[spec(0), spec(1), spec(2), spec(3), vec, vec],
        out_specs=spec(0),
        scratch_shapes=[pltpu.VMEM((HG_HEAD, HG_HEAD), F32)],
        compiler_params=_params("parallel", "parallel", "arbitrary"),
        name="hgrn",
    )(proj, proj, proj, proj, lb.reshape(1, d), norm_w.reshape(1, d))


def _rwkv_layer(h, nw, p, v_first, batch, seq):
    m, d = h.shape
    xn = rmsnorm(h, nw)
    x3 = xn.reshape(batch, seq, d)
    xprev = jnp.pad(x3, ((0, 0), (1, 0), (0, 0)))[:, :-1].reshape(m, d)
    xx = xprev - xn
    xm = [(xn + xx * p["mu"][i]).astype(BF16) for i in range(6)]
    r = matmul(xm[0], p["w_rkv"][0])
    k = matmul(xm[1], p["w_rkv"][1])
    v = matmul(xm[2], p["w_rkv"][2])
    zw = p["w0"] + matmul(jnp.tanh(matmul(xm[3], p["w1"])).astype(BF16), p["w2"])
    ld = -jax.nn.sigmoid(zw) * math.exp(-0.5)
    a = jax.nn.sigmoid(p["a0"] + matmul(matmul(xm[4], p["a1"]).astype(BF16), p["a2"]))
    g = matmul(jax.nn.sigmoid(matmul(xm[5], p["g1"])).astype(BF16), p["g2"])
    if v_first is None:
        v_first = v
    else:
        gate = jax.nn.sigmoid(p["v0"] + matmul(matmul(xm[2], p["v1"]).astype(BF16), p["v2"]))
        v = v + (v_first - v) * gate
    heads = d // RW_HEAD
    kk = (k * p["k_k"]).reshape(m, heads, RW_HEAD)
    kk = kk / jnp.maximum(jnp.sqrt(jnp.sum(kk * kk, axis=-1, keepdims=True)), 1e-12)
    kk = kk.reshape(m, d)
    k = k * (1.0 + (a - 1.0) * p["k_a"])
    y = rwkv_scan(r, ld, k, v, kk, a, batch, seq)
    yh = y.reshape(m, heads, RW_HEAD)
    mu = jnp.mean(yh, axis=-1, keepdims=True)
    var = jnp.mean(jnp.square(yh - mu), axis=-1, keepdims=True)
    yn = ((yh - mu) * lax.rsqrt(var + RW_GN_EPS)).reshape(m, d) * p["lnx_w"] + p["lnx_b"]
    rk = (r * k).reshape(m, heads, RW_HEAD) * p["r_k"]
    bonus = (jnp.sum(rk, axis=-1, keepdims=True) * v.reshape(m, heads, RW_HEAD)).reshape(m, d)
    out = ((yn + bonus) * g).astype(BF16)
    return matmul(out, p["w_o"], res=h), v_first


def kernel(x, norm_mix, norm_ffn, norm_final, ffn_w_up, ffn_conv, ffn_w_down, rw_mu, rw_w_rkv, rw_w0, rw_w1, rw_w2, rw_a0, rw_a1, rw_a2, rw_v0, rw_v1, rw_v2, rw_g1, rw_g2, rw_k_k, rw_k_a, rw_r_k, rw_lnx_w, rw_lnx_b, rw_w_o, ret_w_in, ret_gn_w, ret_w_o, hg_w_in, hg_lb_logits, hg_norm_w, hg_w_o):
    batch, seq, d = x.shape
    depth = norm_mix.shape[0]
    m = batch * seq
    lb_all = jnp.cumsum(jax.nn.softmax(hg_lb_logits.astype(F32), axis=0), axis=0)
    lb_all = lb_all - lb_all[0]
    bf = lambda t: t.astype(BF16)

    h = x.reshape(m, d)
    v_first = None
    for layer in range(depth):
        kind = layer % 3
        j = layer // 3
        if kind == 0:
            p = dict(mu=rw_mu[j], w_rkv=bf(rw_w_rkv[j]), w0=rw_w0[j], w1=bf(rw_w1[j]), w2=bf(rw_w2[j]),
                     a0=rw_a0[j], a1=bf(rw_a1[j]), a2=bf(rw_a2[j]), g1=bf(rw_g1[j]), g2=bf(rw_g2[j]),
                     k_k=rw_k_k[j], k_a=rw_k_a[j], r_k=rw_r_k[j], lnx_w=rw_lnx_w[j], lnx_b=rw_lnx_b[j],
                     w_o=bf(rw_w_o[j]))
            if j > 0:
                p.update(v0=rw_v0[j - 1], v1=bf(rw_v1[j - 1]), v2=bf(rw_v2[j - 1]))
            h, v_first = _rwkv_layer(h, norm_mix[layer], p, v_first, batch, seq)
        elif kind == 1:
            proj = norm_matmul(h, norm_mix[layer], bf(ret_w_in[j]))
            gated = retention(proj, ret_gn_w[j], batch, seq)
            h = matmul(gated, bf(ret_w_o[j]), res=h)
        else:
            proj = norm_matmul(h, norm_mix[layer], bf(hg_w_in[j]))
            gated = hgrn_mix(proj, lb_all[layer], hg_norm_w[j], batch, seq)
            h = matmul(gated, bf(hg_w_o[j]), res=h)
        up, cv, down = _prep_ffn(ffn_w_up[layer], ffn_conv[layer], ffn_w_down[layer])
        h = ffn_block(h, norm_ffn[layer], up, cv, down, seq)
    return rmsnorm(h, norm_final).reshape(batch, seq, d)
```

```python
import functools
import math

import jax
import jax.numpy as jnp
from jax import lax
from jax.experimental import pallas as pl
from jax.experimental.pallas import tpu as pltpu

F32 = jnp.float32
BF16 = jnp.bfloat16

RMS_EPS = 1e-6
GN_EPS = 1e-5
ROPE_BASE = 10000.0

RW_HEAD = 64
RW_GN_EPS = RW_HEAD * 1e-5
RW_CHUNK = 64
RW_GROUP = 4
RW_STEP_GROUPS = 4
RW_ROW_TILE = 256
RW_COL_TILE = 256
EXP_NEG_HALF = math.exp(-0.5)

RET_QK_HEAD = 256
RET_V_HEAD = 512
RET_CHUNK = 128

HG_HEAD = 128
HG_SUB = 16
HG_CHUNK = 64
HG_GROUP = 4

FFN_CONV = 3
LANE = 128
SUBLANE = 8
ROW_TILE = 512
FFN_TN = 512

VMEM_LIMIT = 56 * 1024 * 1024


def _dot(a, b):
    return jnp.dot(a, b, preferred_element_type=F32)


def _dot_nt(a, b):
    return lax.dot_general(a, b, (((1,), (1,)), ((), ())), preferred_element_type=F32)


def _dot_tn(a, b):
    return lax.dot_general(a, b, (((0,), (0,)), ((), ())), preferred_element_type=F32)


def _params(*sem):
    return pltpu.CompilerParams(dimension_semantics=sem, vmem_limit_bytes=VMEM_LIMIT)


def _rms(x, w):
    ms = jnp.mean(x * x, axis=-1, keepdims=True)
    return x * lax.rsqrt(ms + RMS_EPS) * w


def _head_ones(n, head):
    row = lax.broadcasted_iota(jnp.int32, (n, n), 0)
    col = lax.broadcasted_iota(jnp.int32, (n, n), 1)
    return ((row // head) == (col // head)).astype(BF16)


def _split_dot(x, w):
    hi = x.astype(BF16)
    lo = (x - hi.astype(F32)).astype(BF16)
    return _dot(hi, w) + _dot(lo, w)


def _shift_rows(u, carry, fresh):
    c = jnp.where(fresh, 0.0, carry)
    row = lax.broadcasted_iota(jnp.int32, u.shape, 0)
    u1 = jnp.where(row == 0, c[7:8, :], pltpu.roll(u, 1, axis=0))
    u2 = jnp.where(row == 0, c[6:7, :], jnp.where(row == 1, c[7:8, :], pltpu.roll(u, 2, axis=0)))
    return u1, u2


def _mm_res_kernel(x_ref, w_ref, r_ref, o_ref):
    o_ref[...] = (r_ref[...] + _dot(x_ref[...], w_ref[...])).astype(o_ref.dtype)


def _col_tile(n):
    for t in (1024, 512, 256, 128):
        if n % t == 0:
            return t
    return n


def matmul_res(x, w, res):
    m, k = x.shape
    n = w.shape[1]
    tm = min(ROW_TILE, m)
    tn = _col_tile(n)
    return pl.pallas_call(
        _mm_res_kernel,
        out_shape=jax.ShapeDtypeStruct((m, n), F32),
        grid=(m // tm, n // tn),
        in_specs=[pl.BlockSpec((tm, k), lambda i, j: (i, 0)),
                  pl.BlockSpec((k, tn), lambda i, j: (0, j)),
                  pl.BlockSpec((tm, tn), lambda i, j: (i, j))],
        out_specs=pl.BlockSpec((tm, tn), lambda i, j: (i, j)),
        compiler_params=_params("parallel", "parallel"),
        name="matmul_res",
    )(x, w, res)


def _norm_mm_kernel(h_ref, nw_ref, w_ref, o_ref, xs_ref):
    @pl.when(pl.program_id(1) == 0)
    def _():
        xs_ref[...] = _rms(h_ref[...], nw_ref[...]).astype(BF16)

    o_ref[...] = _dot(xs_ref[...], w_ref[...]).astype(o_ref.dtype)


def norm_matmul(h, nw, w, out_dtype=F32):
    m, d = h.shape
    n = w.shape[1]
    tm = min(ROW_TILE, m)
    tn = _col_tile(n)
    return pl.pallas_call(
        _norm_mm_kernel,
        out_shape=jax.ShapeDtypeStruct((m, n), out_dtype),
        grid=(m // tm, n // tn),
        in_specs=[pl.BlockSpec((tm, d), lambda i, j: (i, 0)),
                  pl.BlockSpec((1, d), lambda i, j: (0, 0)),
                  pl.BlockSpec((d, tn), lambda i, j: (0, j))],
        out_specs=pl.BlockSpec((tm, tn), lambda i, j: (i, j)),
        scratch_shapes=[pltpu.VMEM((tm, d), BF16)],
        compiler_params=_params("parallel", "arbitrary"),
        name="norm_matmul",
    )(h, nw.reshape(1, d), w)


def _ffn_kernel(h_ref, nw_ref, wg_ref, wv_ref, cg_ref, cv_ref, wd_ref, fw_ref, o_ref,
                xs_ref, carry_g_ref, carry_v_ref, *, tiles_per_seq, final_norm):
    i = pl.program_id(0)
    j = pl.program_id(1)

    @pl.when(j == 0)
    def _():
        h = h_ref[...]
        xs_ref[...] = _rms(h, nw_ref[...]).astype(BF16)
        o_ref[...] = h

    fresh = (i % tiles_per_seq) == 0
    xs = xs_ref[...]

    def conv(w_ref, c_ref, carry_ref):
        u = _dot(xs, w_ref[...])
        u1, u2 = _shift_rows(u, carry_ref[j], fresh)
        carry_ref[j] = u[u.shape[0] - SUBLANE:, :]
        cw = c_ref[...]
        return u2 * cw[0:1, :] + u1 * cw[1:2, :] + u * cw[2:3, :]

    gate = conv(wg_ref, cg_ref, carry_g_ref)
    val = conv(wv_ref, cv_ref, carry_v_ref)
    act = (gate * jax.nn.sigmoid(gate) * val).astype(BF16)
    o_ref[...] += _dot(act, wd_ref[...])

    if final_norm:
        @pl.when(j == pl.num_programs(1) - 1)
        def _():
            o_ref[...] = _rms(o_ref[...], fw_ref[...])


def ffn_block(h, nw, w_up, w_conv, w_down, seq, final_w=None):
    m, d = h.shape
    f = w_down.shape[0]
    tm = min(ROW_TILE, seq)
    tn = FFN_TN
    nt = f // tn
    kern = functools.partial(_ffn_kernel, tiles_per_seq=seq // tm, final_norm=final_w is not None)
    fw = (nw if final_w is None else final_w).reshape(1, d)
    vec = pl.BlockSpec((1, d), lambda i, j: (0, 0))
    return pl.pallas_call(
        kern,
        out_shape=jax.ShapeDtypeStruct((m, d), F32),
        grid=(m // tm, nt),
        in_specs=[pl.BlockSpec((tm, d), lambda i, j: (i, 0)),
                  vec,
                  pl.BlockSpec((d, tn), lambda i, j: (0, j)),
                  pl.BlockSpec((d, tn), lambda i, j: (0, j + nt)),
                  pl.BlockSpec((FFN_CONV, tn), lambda i, j: (0, j)),
                  pl.BlockSpec((FFN_CONV, tn), lambda i, j: (0, j + nt)),
                  pl.BlockSpec((tn, d), lambda i, j: (j, 0)),
                  vec],
        out_specs=pl.BlockSpec((tm, d), lambda i, j: (i, 0)),
        scratch_shapes=[pltpu.VMEM((tm, d), BF16),
                        pltpu.VMEM((nt, SUBLANE, tn), F32),
                        pltpu.VMEM((nt, SUBLANE, tn), F32)],
        compiler_params=_params("arbitrary", "arbitrary"),
        name="ffn",
    )(h, nw.reshape(1, d), w_up, w_up, w_conv, w_conv, w_down, fw)


def _prep_ffn(w_up, w_conv, w_down):
    f = w_down.shape[0]
    fp = -(-f // FFN_TN) * FFN_TN
    pad = fp - f
    up = jnp.concatenate([jnp.pad(w_up[:, :f], ((0, 0), (0, pad))),
                          jnp.pad(w_up[:, f:], ((0, 0), (0, pad)))], axis=1).astype(BF16)
    cv = jnp.concatenate([jnp.pad(w_conv[:, :f], ((0, 0), (0, pad))),
                          jnp.pad(w_conv[:, f:], ((0, 0), (0, pad)))], axis=1)
    down = jnp.pad(w_down, ((0, pad), (0, 0))).astype(BF16)
    return up, cv, down


def _rwkv_proj_kernel(*refs, vres, tiles_per_seq):
    refs = list(refs)
    h_ref, nw_ref, mu_ref, wr_ref, wk_ref, wv_ref, w1_ref, a1_ref, g1_ref = refs[:9]
    w2_ref, a2_ref, g2_ref, vec_ref = refs[9:13]
    pos = 13
    if vres:
        v1_ref, v2_ref, vf_ref = refs[pos:pos + 3]
        pos += 3
    r_ref, k_ref, v_ref, ld_ref, kk_ref, a_ref, g_ref = refs[pos:pos + 7]
    pos += 7
    xm_ref, hw_ref, ha_ref, hg_ref = refs[pos:pos + 4]
    pos += 4
    if vres:
        hv_ref = refs[pos]
        pos += 1
    carry_ref = refs[pos]

    i = pl.program_id(0)

    @pl.when(pl.program_id(1) == 0)
    def _():
        xn = _rms(h_ref[...], nw_ref[...])
        tm = xn.shape[0]
        fresh = (i % tiles_per_seq) == 0
        prev_last = jnp.where(fresh, 0.0, carry_ref[SUBLANE - 1:SUBLANE, :])
        row = lax.broadcasted_iota(jnp.int32, xn.shape, 0)
        xprev = jnp.where(row == 0, prev_last, pltpu.roll(xn, 1, axis=0))
        carry_ref[...] = xn[tm - SUBLANE:, :]
        xx = xprev - xn
        mu = mu_ref[...]
        for q in range(6):
            xm_ref[q] = (xn + xx * mu[q:q + 1, :]).astype(BF16)
        hw_ref[...] = jnp.tanh(_dot(xm_ref[3], w1_ref[...])).astype(BF16)
        ha_ref[...] = _dot(xm_ref[4], a1_ref[...]).astype(BF16)
        hg_ref[...] = jax.nn.sigmoid(_dot(xm_ref[5], g1_ref[...])).astype(BF16)
        if vres:
            hv_ref[...] = _dot(xm_ref[2], v1_ref[...]).astype(BF16)

    vec = vec_ref[...]
    w0, a0, k_k, k_a, v0 = (vec[q:q + 1, :] for q in range(5))
    r = _dot(xm_ref[0], wr_ref[...])
    k = _dot(xm_ref[1], wk_ref[...])
    v = _dot(xm_ref[2], wv_ref[...])
    ld = -jax.nn.sigmoid(w0 + _dot(hw_ref[...], w2_ref[...])) * EXP_NEG_HALF
    a = jax.nn.sigmoid(a0 + _dot(ha_ref[...], a2_ref[...]))
    g = _dot(hg_ref[...], g2_ref[...])
    if vres:
        v = v + (vf_ref[...] - v) * jax.nn.sigmoid(v0 + _dot(hv_ref[...], v2_ref[...]))
    kkr = k * k_k
    ss = _split_dot(kkr * kkr, _head_ones(kkr.shape[1], RW_HEAD))
    r_ref[...] = r
    k_ref[...] = k * (1.0 + (a - 1.0) * k_a)
    v_ref[...] = v
    ld_ref[...] = ld
    kk_ref[...] = kkr / jnp.maximum(jnp.sqrt(ss), 1e-12)
    a_ref[...] = a
    g_ref[...] = g.astype(g_ref.dtype)


def _pad_cols(w, n):
    return jnp.pad(w, ((0, 0), (0, n - w.shape[1])))


def _pad_rows(w, n):
    return jnp.pad(w, ((0, n - w.shape[0]), (0, 0)))


def rwkv_proj(h, nw, p, v_first, seq):
    m, d = h.shape
    tm = min(RW_ROW_TILE, seq)
    tn = min(RW_COL_TILE, d)
    vres = v_first is not None
    lw, la, lg = (-(-p[n].shape[1] // LANE) * LANE for n in ("w1", "a1", "g1"))
    full = lambda shape: pl.BlockSpec(shape, lambda i, j: (0,) * len(shape))
    col = lambda rows: pl.BlockSpec((rows, tn), lambda i, j: (0, j))
    tile = pl.BlockSpec((tm, tn), lambda i, j: (i, j))
    rkv = lambda q: pl.BlockSpec((None, d, tn), lambda i, j: (q, 0, j))
    zeros = jnp.zeros((d,), F32)
    vec = jnp.stack([p["w0"], p["a0"], p["k_k"], p["k_a"], p["v0"] if vres else zeros,
                     zeros, zeros, zeros])
    mu = jnp.pad(p["mu"], ((0, SUBLANE - p["mu"].shape[0]), (0, 0)))
    args = [h, nw.reshape(1, d), mu, p["w_rkv"], p["w_rkv"], p["w_rkv"],
            _pad_cols(p["w1"], lw), _pad_cols(p["a1"], la), _pad_cols(p["g1"], lg),
            _pad_rows(p["w2"], lw), _pad_rows(p["a2"], la), _pad_rows(p["g2"], lg), vec]
    in_specs = [pl.BlockSpec((tm, d), lambda i, j: (i, 0)), full((1, d)), full((SUBLANE, d)),
                rkv(0), rkv(1), rkv(2), full((d, lw)), full((d, la)), full((d, lg)),
                col(lw), col(la), col(lg), col(SUBLANE)]
    scratch = [pltpu.VMEM((6, tm, d), BF16), pltpu.VMEM((tm, lw), BF16),
               pltpu.VMEM((tm, la), BF16), pltpu.VMEM((tm, lg), BF16)]
    if vres:
        lv = -(-p["v1"].shape[1] // LANE) * LANE
        args += [_pad_cols(p["v1"], lv), _pad_rows(p["v2"], lv), v_first]
        in_specs += [full((d, lv)), col(lv), tile]
        scratch.append(pltpu.VMEM((tm, lv), BF16))
    scratch.append(pltpu.VMEM((SUBLANE, d), F32))
    out = jax.ShapeDtypeStruct((m, d), F32)
    kern = functools.partial(_rwkv_proj_kernel, vres=vres, tiles_per_seq=seq // tm)
    return pl.pallas_call(
        kern,
        out_shape=[out] * 6 + [jax.ShapeDtypeStruct((m, d), BF16)],
        grid=(m // tm, d // tn),
        in_specs=in_specs,
        out_specs=[tile] * 7,
        scratch_shapes=scratch,
        compiler_params=_params("arbitrary", "arbitrary"),
        name="rwkv_proj",
    )(*args)


def _rwkv_group(r, ld, k, v, kk, a, z, masks):
    tri, bdm, strict, incl, eye = masks
    C, W = r[0].shape
    G = W // RW_HEAD
    bfl = lambda xs: [x.astype(BF16) for x in xs]

    def each(f, *ls):
        return [f(*xs) for xs in zip(*ls)]

    def bd(xs):
        return [jnp.where(bdm, jnp.concatenate([x.astype(BF16)] * G, axis=0), 0) for x in xs]

    cum = each(lambda x: _split_dot_left(tri, x), ld)
    last = [c[C - 1:C, :] for c in cum]
    b = each(jnp.multiply, kk, a)
    einv = [jnp.exp(-c) for c in cum]
    etail = each(lambda l, c: jnp.exp(l - c), last, cum)
    at = each(lambda x, c, l: -x * jnp.exp(c - l), kk, cum, ld)
    rt = each(lambda x, c: x * jnp.exp(c), r, cum)
    bt = each(jnp.multiply, b, einv)
    kt = each(jnp.multiply, k, einv)
    bh = each(jnp.multiply, b, etail)
    kh = each(jnp.multiply, k, etail)

    vb = bd(v)
    lhs = each(lambda x, y: jnp.concatenate([x, y], axis=0).astype(BF16), at, rt)
    rhs = each(lambda x, y: jnp.concatenate([x, y], axis=0), bd(bt), bd(kt))
    gram = each(_dot_nt, lhs, rhs)
    a_ab = [jnp.where(strict, x[:C, :W], 0.0) for x in gram]
    a_ak = [jnp.where(strict, x[:C, W:], 0.0).astype(BF16) for x in gram]
    a_rb = [jnp.where(incl, x[C:, :W], 0.0).astype(BF16) for x in gram]
    a_rk = [jnp.where(incl, x[C:, W:], 0.0).astype(BF16) for x in gram]

    p = a_ab
    t = [eye + x for x in a_ab]
    pb = bd(p)
    for _ in range(int(math.log2(C)) - 1):
        p = each(_dot, bfl(p), pb)
        pb = bd(p)
        t = each(lambda x, y: x + _dot(x.astype(BF16), y), t, pb)
    tb = bfl(t)

    wa = each(_dot, tb, bd(at))
    x1 = each(_dot, a_ak, vb)
    uv = each(_dot, tb, bd(x1))
    qp = each(lambda x, y, w: x + _dot(y, w), rt, a_rb, bd(wa))
    yv = each(lambda x, u, y, w: _dot(x, u) + _dot(y, w), a_rb, bd(uv), a_rk, vb)

    zb = bfl(z)
    y = each(lambda q, s, w: _dot_nt(q.astype(BF16), s) + w, qp, zb, yv)
    mc = each(lambda w, x: jnp.where(bdm, _dot_tn(w.astype(BF16), x.astype(BF16)), 0.0), wa, bh)
    gm = each(lambda u, x, w, s: jnp.where(
        bdm, _dot_tn(jnp.concatenate([u, x], axis=0).astype(BF16),
                     jnp.concatenate([w, s], axis=0).astype(BF16)), 0.0), uv, v, bh, kh)
    zn = each(lambda s, l, sb, m_, g_: s * jnp.exp(l) + _dot(sb, m_.astype(BF16)) + g_,
              z, last, zb, mc, gm)
    return y, zn


def _split_dot_left(w, x):
    hi = x.astype(BF16)
    lo = (x - hi.astype(F32)).astype(BF16)
    return _dot(w, hi) + _dot(w, lo)


def _rwkv_masks(C, W):
    trow = lax.broadcasted_iota(jnp.int32, (C, C), 0)
    tcol = lax.broadcasted_iota(jnp.int32, (C, C), 1)
    tri = (trow >= tcol).astype(BF16)
    brow = lax.broadcasted_iota(jnp.int32, (W, W), 0)
    bcol = lax.broadcasted_iota(jnp.int32, (W, W), 1)
    bdm = (brow // C) == (bcol // RW_HEAD)
    lrow = lax.broadcasted_iota(jnp.int32, (C, W), 0)
    lcol = lax.broadcasted_iota(jnp.int32, (C, W), 1) % RW_HEAD
    eye = jnp.where(lrow == lcol, 1.0, 0.0)
    return tri, bdm, lrow > lcol, lrow >= lcol, eye


def _rwkv_chunk_kernel(r_ref, ld_ref, k_ref, v_ref, kk_ref, a_ref, g_ref, vec_ref, o_ref, z_ref):
    @pl.when(pl.program_id(2) == 0)
    def _():
        z_ref[...] = jnp.zeros_like(z_ref)

    C = r_ref.shape[0]
    W = RW_GROUP * RW_HEAD
    masks = _rwkv_masks(C, W)
    hsum = _head_ones(W, RW_HEAD)
    inv_n = 1.0 / RW_HEAD
    ng = r_ref.shape[1] // W
    sls = [slice(gi * W, (gi + 1) * W) for gi in range(ng)]
    take = lambda ref: [ref[:, sl] for sl in sls]
    r, k, v = take(r_ref), take(k_ref), take(v_ref)
    ys, zs = _rwkv_group(r, take(ld_ref), k, v, take(kk_ref), take(a_ref),
                         [z_ref[gi] for gi in range(ng)], masks)
    for gi in range(ng):
        z_ref[gi] = zs[gi]
    mean = [_dot(y.astype(BF16), hsum) * inv_n for y in ys]
    yc = [y - mu for y, mu in zip(ys, mean)]
    var = [_dot((x * x).astype(BF16), hsum) * inv_n for x in yc]
    for gi, sl in enumerate(sls):
        vec = vec_ref[:, sl]
        r_k, lnx_w, lnx_b = vec[0:1, :], vec[1:2, :], vec[2:3, :]
        yn = yc[gi] * lax.rsqrt(var[gi] + RW_GN_EPS) * lnx_w + lnx_b
        bonus = _dot((r[gi] * k[gi] * r_k).astype(BF16), hsum) * v[gi]
        o_ref[:, sl] = ((yn + bonus) * g_ref[:, sl].astype(F32)).astype(o_ref.dtype)


def rwkv_scan(r, ld, k, v, kk, a, g, r_k, lnx_w, lnx_b, batch, seq):
    m, d = r.shape
    C = RW_CHUNK
    W = RW_GROUP * RW_HEAD
    assert RW_GROUP * C == W
    wb = min(RW_STEP_GROUPS * W, d)
    nc = seq // C
    zeros = jnp.zeros((d,), F32)
    vec = jnp.stack([r_k.reshape(d), lnx_w, lnx_b] + [zeros] * (SUBLANE - 3))
    spec = pl.BlockSpec((C, wb), lambda b, g, c: (b * nc + c, g))
    return pl.pallas_call(
        _rwkv_chunk_kernel,
        out_shape=jax.ShapeDtypeStruct((m, d), BF16),
        grid=(batch, d // wb, nc),
        in_specs=[spec] * 7 + [pl.BlockSpec((SUBLANE, wb), lambda b, g, c: (0, g))],
        out_specs=spec,
        scratch_shapes=[pltpu.VMEM((wb // W, W, W), F32)],
        compiler_params=_params("parallel", "parallel", "arbitrary"),
        name="rwkv_scan",
    )(r, ld, k, v, kk, a, g, vec)


def _retention_kernel(q_ref, k_ref, v_ref, g_ref, cos_ref, sin_ref, intra_ref, cross_ref,
                      tail_ref, gnw_ref, o_ref, st_ref):
    c = pl.program_id(2)

    @pl.when(c == 0)
    def _():
        st_ref[...] = jnp.zeros_like(st_ref)

    C = q_ref.shape[0]
    half = RET_QK_HEAD // 2
    cos = cos_ref[...]
    sin = sin_ref[...]

    def rope(t):
        t1 = t[:, :half]
        t2 = t[:, half:]
        return jnp.concatenate([t1 * cos - t2 * sin, t1 * sin + t2 * cos], axis=-1)

    q = rope(q_ref[...])
    k = rope(k_ref[...]) * (RET_QK_HEAD ** -0.5)
    vb = v_ref[...].astype(BF16)
    qb = q.astype(BF16)
    cross = cross_ref[...]
    scores = _dot_nt(qb, k.astype(BF16)) * intra_ref[...]
    st = st_ref[...]
    o = _dot(scores.astype(BF16), vb) + _dot(qb, st.astype(BF16)) * cross
    st_ref[...] = st * cross[C - 1:C, :] + _dot_tn((k * tail_ref[...]).astype(BF16), vb)

    mu = jnp.mean(o, axis=-1, keepdims=True)
    oc = o - mu
    var = jnp.mean(oc * oc, axis=-1, keepdims=True)
    on = oc * lax.rsqrt(var + GN_EPS) * gnw_ref[...]
    g = g_ref[...]
    o_ref[...] = (g * jax.nn.sigmoid(g) * on).astype(o_ref.dtype)


def retention(proj, gn_w, batch, seq):
    m, n = proj.shape
    heads = n // (2 * RET_QK_HEAD + 2 * RET_V_HEAD)
    C = RET_CHUNK
    nc = seq // C
    half = RET_QK_HEAD // 2
    pos = jnp.arange(seq, dtype=F32)
    inv_freq = 1.0 / (ROPE_BASE ** (jnp.arange(0, RET_QK_HEAD, 2, dtype=F32) / RET_QK_HEAD))
    ang = pos[:, None] * inv_freq[None, :]
    cos, sin = jnp.cos(ang), jnp.sin(ang)
    log_gamma = jnp.log1p(-jnp.exp2(-5.0 - jnp.arange(heads, dtype=F32)))
    cp = jnp.arange(C, dtype=F32)
    rel = cp[:, None] - cp[None, :]
    intra = jnp.where(rel >= 0, jnp.exp(log_gamma[:, None, None] * jnp.maximum(rel, 0.0)), 0.0)
    cross = jnp.exp(log_gamma[:, None] * (cp + 1.0))[:, :, None]
    tail = jnp.exp(log_gamma[:, None] * (C - 1.0 - cp))[:, :, None]
    vdim = heads * RET_V_HEAD
    voff = 2 * heads * RET_QK_HEAD // RET_V_HEAD
    return pl.pallas_call(
        _retention_kernel,
        out_shape=jax.ShapeDtypeStruct((m, vdim), BF16),
        grid=(batch, heads, nc),
        in_specs=[pl.BlockSpec((C, RET_QK_HEAD), lambda b, h, c: (b * nc + c, h)),
                  pl.BlockSpec((C, RET_QK_HEAD), lambda b, h, c: (b * nc + c, heads + h)),
                  pl.BlockSpec((C, RET_V_HEAD), lambda b, h, c: (b * nc + c, voff + h)),
                  pl.BlockSpec((C, RET_V_HEAD), lambda b, h, c: (b * nc + c, voff + heads + h)),
                  pl.BlockSpec((C, half), lambda b, h, c: (c, 0)),
                  pl.BlockSpec((C, half), lambda b, h, c: (c, 0)),
                  pl.BlockSpec((None, C, C), lambda b, h, c: (h, 0, 0)),
                  pl.BlockSpec((None, C, 1), lambda b, h, c: (h, 0, 0)),
                  pl.BlockSpec((None, C, 1), lambda b, h, c: (h, 0, 0)),
                  pl.BlockSpec((1, RET_V_HEAD), lambda b, h, c: (0, h))],
        out_specs=pl.BlockSpec((C, RET_V_HEAD), lambda b, h, c: (b * nc + c, h)),
        scratch_shapes=[pltpu.VMEM((RET_QK_HEAD, RET_V_HEAD), F32)],
        compiler_params=_params("parallel", "parallel", "arbitrary"),
        name="retention",
    )(proj, proj, proj, proj, cos, sin, intra, cross, tail, gn_w.reshape(1, vdim))


def _hgrn_kernel(q_ref, f_ref, i_ref, g_ref, lb_ref, nw_ref, o_ref, zt_ref):
    @pl.when(pl.program_id(2) == 0)
    def _():
        zt_ref[...] = jnp.zeros_like(zt_ref)

    C, wd = q_ref.shape
    nh = wd // HG_HEAD
    B = HG_SUB
    lb = lb_ref[...]
    qr = q_ref[...]
    v = i_ref[...]
    q = qr * jax.nn.sigmoid(qr)
    sig = jax.nn.sigmoid(f_ref[...])
    lf = jnp.log(lb + (1.0 - lb) * sig)
    k = (1.0 - lb) * (1.0 - sig)

    trow = lax.broadcasted_iota(jnp.int32, (C, C), 0)
    tcol = lax.broadcasted_iota(jnp.int32, (C, C), 1)
    tri = ((trow >= tcol) & ((trow // B) == (tcol // B))).astype(BF16)
    cum = _split_dot_left(tri, lf)

    m_idx = lax.broadcasted_iota(jnp.int32, (B, B, wd), 0)
    n_idx = lax.broadcasted_iota(jnp.int32, (B, B, wd), 1)
    causal = n_idx >= m_idx
    ones = jnp.ones((HG_HEAD, HG_HEAD), BF16)
    gate = nw_ref[...] * jax.nn.sigmoid(g_ref[...])

    zts = [zt_ref[hh] for hh in range(nh)]
    for s in range(C // B):
        rows = slice(s * B, (s + 1) * B)
        cs = cum[rows]
        last = cs[B - 1:B, :]
        qs, ks, vs = q[rows], k[rows], v[rows]
        qe = (qs * jnp.exp(cs)).astype(BF16)
        kd = (ks * jnp.exp(last - cs)).astype(BF16)
        dec = jnp.exp(last)
        e = jnp.exp(jnp.where(causal, cs[None, :, :] - cs[:, None, :], -jnp.inf))
        p = (qs[None, :, :] * ks[:, None, :] * e).reshape(B * B, wd).astype(BF16)
        vs_b = vs.astype(BF16)
        outs = []
        for hh in range(nh):
            ls = slice(hh * HG_HEAD, (hh + 1) * HG_HEAD)
            zt = zts[hh]
            o = _dot_nt(qe[:, ls], zt.astype(BF16))
            srep = _dot(p[:, ls], ones).reshape(B, B, HG_HEAD)
            o = o + jnp.sum(srep * vs[:, ls][:, None, :], axis=0)
            zts[hh] = zt * dec[:, ls] + _dot_tn(vs_b[:, ls], kd[:, ls])
            outs.append(o * lax.rsqrt(jnp.mean(o * o, axis=-1, keepdims=True) + RMS_EPS))
        o_ref[rows, :] = (jnp.concatenate(outs, axis=-1) * gate[rows]).astype(o_ref.dtype)
    for hh in range(nh):
        zt_ref[hh] = zts[hh]


def hgrn_mix(proj, lb, norm_w, batch, seq):
    m, n = proj.shape
    d = n // 4
    wb = min(HG_GROUP * HG_HEAD, d)
    nb = d // wb
    C = HG_CHUNK
    nc = seq // C

    def spec(off):
        return pl.BlockSpec((C, wb), lambda b, h, c: (b * nc + c, off * nb + h))

    vec = pl.BlockSpec((1, wb), lambda b, h, c: (0, h))
    return pl.pallas_call(
        _hgrn_kernel,
        out_shape=jax.ShapeDtypeStruct((m, d), BF16),
        grid=(batch, nb, nc),
        in_specs=[spec(0), spec(1), spec(2), spec(3), vec, vec],
        out_specs=spec(0),
        scratch_shapes=[pltpu.VMEM((wb // HG_HEAD, HG_HEAD, HG_HEAD), F32)],
        compiler_params=_params("parallel", "parallel", "arbitrary"),
        name="hgrn",
    )(proj, proj, proj, proj, lb.reshape(1, d), norm_w.reshape(1, d))


def kernel(x, norm_mix, norm_ffn, norm_final, ffn_w_up, ffn_conv, ffn_w_down, rw_mu, rw_w_rkv, rw_w0, rw_w1, rw_w2, rw_a0, rw_a1, rw_a2, rw_v0, rw_v1, rw_v2, rw_g1, rw_g2, rw_k_k, rw_k_a, rw_r_k, rw_lnx_w, rw_lnx_b, rw_w_o, ret_w_in, ret_gn_w, ret_w_o, hg_w_in, hg_lb_logits, hg_norm_w, hg_w_o):
    batch, seq, d = x.shape
    depth = norm_mix.shape[0]
    m = batch * seq
    lb_all = jnp.cumsum(jax.nn.softmax(hg_lb_logits.astype(F32), axis=0), axis=0)
    lb_all = lb_all - lb_all[0]
    bf = lambda t: t.astype(BF16)

    h = x.reshape(m, d)
    v_first = None
    for layer in range(depth):
        kind = layer % 3
        j = layer // 3
        if kind == 0:
            p = dict(mu=rw_mu[j], w_rkv=bf(rw_w_rkv[j]), w0=rw_w0[j], w1=bf(rw_w1[j]), w2=bf(rw_w2[j]),
                     a0=rw_a0[j], a1=bf(rw_a1[j]), a2=bf(rw_a2[j]), g1=bf(rw_g1[j]), g2=bf(rw_g2[j]),
                     k_k=rw_k_k[j], k_a=rw_k_a[j])
            if j > 0:
                p.update(v0=rw_v0[j - 1], v1=bf(rw_v1[j - 1]), v2=bf(rw_v2[j - 1]))
            r, k, v, ld, kk, a, g = rwkv_proj(h, norm_mix[layer], p, v_first, seq)
            if v_first is None:
                v_first = v
            mixed = rwkv_scan(r, ld, k, v, kk, a, g, rw_r_k[j], rw_lnx_w[j], rw_lnx_b[j], batch, seq)
            h = matmul_res(mixed, bf(rw_w_o[j]), h)
        elif kind == 1:
            proj = norm_matmul(h, norm_mix[layer], bf(ret_w_in[j]))
            gated = retention(proj, ret_gn_w[j], batch, seq)
            h = matmul_res(gated, bf(ret_w_o[j]), h)
        else:
            proj = norm_matmul(h, norm_mix[layer], bf(hg_w_in[j]))
            gated = hgrn_mix(proj, lb_all[layer], hg_norm_w[j], batch, seq)
            h = matmul_res(gated, bf(hg_w_o[j]), h)
        up, cv, down = _prep_ffn(ffn_w_up[layer], ffn_conv[layer], ffn_w_down[layer])
        final_w = norm_final if layer == depth - 1 else None
        h = ffn_block(h, norm_ffn[layer], up, cv, down, seq, final_w)
    return h.reshape(batch, seq, d)
```

```python
import functools
import math

import jax
import jax.numpy as jnp
from jax import lax
from jax.experimental import pallas as pl
from jax.experimental.pallas import tpu as pltpu

F32 = jnp.float32
BF16 = jnp.bfloat16

RMS_EPS = 1e-6
GN_EPS = 1e-5
ROPE_BASE = 10000.0

RW_HEAD = 64
RW_GN_EPS = RW_HEAD * 1e-5
RW_CHUNK = 64
RW_GROUP = 4
RW_STEP_GROUPS = 8
RW_ROW_TILE = 512
RW_PRE_ROWS = 128
RW_COL_TILE = 256
EXP_NEG_HALF = math.exp(-0.5)
LOG2_E = 1.0 / math.log(2.0)

RET_QK_HEAD = 256
RET_V_HEAD = 512
RET_CHUNK = 128
RET_STEP_HEADS = 4

HG_HEAD = 128
HG_SUB = 16
HG_CHUNK = 64
HG_GROUP = 4

FFN_CONV = 3
LANE = 128
SUBLANE = 8
ROW_TILE = 512
MM_ROW_TILE = 1024
MM_TILE_ELEMS = 2 * 1024 * 1024
NORM_ROWS = 256
FFN_TN = 512
FFN_SUB = 512
FFN_ROWS = 128

VMEM_LIMIT = 56 * 1024 * 1024


def _dot(a, b):
    return jnp.dot(a, b, preferred_element_type=F32)


def _dot_nt(a, b):
    return lax.dot_general(a, b, (((1,), (1,)), ((), ())), preferred_element_type=F32)


def _dot_tn(a, b):
    return lax.dot_general(a, b, (((0,), (0,)), ((), ())), preferred_element_type=F32)


def _params(*sem):
    return pltpu.CompilerParams(dimension_semantics=sem, vmem_limit_bytes=VMEM_LIMIT)


def _rms(x, w):
    ms = jnp.mean(x * x, axis=-1, keepdims=True)
    return x * lax.rsqrt(ms + RMS_EPS) * w


def _head_ones(n, head):
    row = lax.broadcasted_iota(jnp.int32, (n, n), 0)
    col = lax.broadcasted_iota(jnp.int32, (n, n), 1)
    return ((row // head) == (col // head)).astype(BF16)


def _split_dot(x, w):
    hi = x.astype(BF16)
    lo = (x - hi.astype(F32)).astype(BF16)
    return _dot(hi, w) + _dot(lo, w)


def _shift_rows(u, carry, fresh):
    c = jnp.where(fresh, 0.0, carry)
    row = lax.broadcasted_iota(jnp.int32, u.shape, 0)
    u1 = jnp.where(row == 0, c[7:8, :], pltpu.roll(u, 1, axis=0))
    u2 = jnp.where(row == 0, c[6:7, :], jnp.where(row == 1, c[7:8, :], pltpu.roll(u, 2, axis=0)))
    return u1, u2


def _mm_res_kernel(x_ref, w_ref, r_ref, o_ref):
    o_ref[...] = (r_ref[...] + _dot(x_ref[...], w_ref[...])).astype(o_ref.dtype)


def _col_tile(n):
    for t in (1024, 512, 256, 128):
        if n % t == 0:
            return t
    return n


def matmul_res(x, w, res):
    m, k = x.shape
    n = w.shape[1]
    tm = min(MM_ROW_TILE, m)
    tn = min(_col_tile(n), MM_TILE_ELEMS // k)
    return pl.pallas_call(
        _mm_res_kernel,
        out_shape=jax.ShapeDtypeStruct((m, n), F32),
        grid=(m // tm, n // tn),
        in_specs=[pl.BlockSpec((tm, k), lambda i, j: (i, 0)),
                  pl.BlockSpec((k, tn), lambda i, j: (0, j)),
                  pl.BlockSpec((tm, tn), lambda i, j: (i, j))],
        out_specs=pl.BlockSpec((tm, tn), lambda i, j: (i, j)),
        compiler_params=_params("parallel", "parallel"),
        name="matmul_res",
    )(x, w, res)


def _norm_mm_kernel(h_ref, nw_ref, w_ref, o_ref, xs_ref):
    @pl.when(pl.program_id(1) == 0)
    def _():
        tm = h_ref.shape[0]
        rb = min(NORM_ROWS, tm)
        nw = nw_ref[...]
        for r0 in range(0, tm, rb):
            xs_ref[r0:r0 + rb, :] = _rms(h_ref[r0:r0 + rb, :], nw).astype(BF16)

    o_ref[...] = _dot(xs_ref[...], w_ref[...]).astype(o_ref.dtype)


def norm_matmul(h, nw, w, out_dtype=F32):
    m, d = h.shape
    n = w.shape[1]
    tm = min(MM_ROW_TILE, m)
    tn = _col_tile(n)
    return pl.pallas_call(
        _norm_mm_kernel,
        out_shape=jax.ShapeDtypeStruct((m, n), out_dtype),
        grid=(m // tm, n // tn),
        in_specs=[pl.BlockSpec((tm, d), lambda i, j: (i, 0)),
                  pl.BlockSpec((1, d), lambda i, j: (0, 0)),
                  pl.BlockSpec((d, tn), lambda i, j: (0, j))],
        out_specs=pl.BlockSpec((tm, tn), lambda i, j: (i, j)),
        scratch_shapes=[pltpu.VMEM((tm, d), BF16)],
        compiler_params=_params("parallel", "arbitrary"),
        name="norm_matmul",
    )(h, nw.reshape(1, d), w)


def _ffn_kernel(h_ref, nw_ref, wg_ref, wv_ref, cg_ref, cv_ref, wd_ref, fw_ref, o_ref,
                xs_ref, carry_g_ref, carry_v_ref, *, tiles_per_seq, final_norm):
    i = pl.program_id(0)
    j = pl.program_id(1)

    @pl.when(j == 0)
    def _():
        h = h_ref[...]
        xs_ref[...] = _rms(h, nw_ref[...]).astype(BF16)
        o_ref[...] = h

    fresh = (i % tiles_per_seq) == 0
    tm = xs_ref.shape[0]
    tn = wg_ref.shape[1]
    sub = min(FFN_SUB, tn)
    parts = [slice(s * sub, (s + 1) * sub) for s in range(tn // sub)]

    def conv(u, cw, carry, first):
        u1, u2 = _shift_rows(u, carry, first)
        return u2 * cw[0:1, :] + u1 * cw[1:2, :] + u * cw[2:3, :]

    cwg, cwv = cg_ref[...], cv_ref[...]
    carry_g, carry_v = carry_g_ref[j], carry_v_ref[j]
    rsub = min(FFN_ROWS, tm)

    acc = None
    for cols in parts:
        cg, cv = carry_g[:, cols], carry_v[:, cols]
        acts = []
        for r0 in range(0, tm, rsub):
            xr = xs_ref[r0:r0 + rsub, :]
            ug, uv = _dot(xr, wg_ref[:, cols]), _dot(xr, wv_ref[:, cols])
            first = fresh if r0 == 0 else False
            gate = conv(ug, cwg[:, cols], cg, first)
            val = conv(uv, cwv[:, cols], cv, first)
            cg, cv = ug[rsub - SUBLANE:, :], uv[rsub - SUBLANE:, :]
            acts.append((gate * jax.nn.sigmoid(gate) * val).astype(BF16))
        carry_g_ref[j, :, cols] = cg
        carry_v_ref[j, :, cols] = cv
        part = _dot(jnp.concatenate(acts, axis=0), wd_ref[cols, :])
        acc = part if acc is None else acc + part
    o_ref[...] += acc

    if final_norm:
        @pl.when(j == pl.num_programs(1) - 1)
        def _():
            o_ref[...] = _rms(o_ref[...], fw_ref[...])


def ffn_block(h, nw, w_up, w_conv, w_down, seq, final_w=None):
    m, d = h.shape
    f = w_down.shape[0]
    tm = min(ROW_TILE, seq)
    tn = FFN_TN
    nt = f // tn
    kern = functools.partial(_ffn_kernel, tiles_per_seq=seq // tm, final_norm=final_w is not None)
    fw = (nw if final_w is None else final_w).reshape(1, d)
    vec = pl.BlockSpec((1, d), lambda i, j: (0, 0))
    return pl.pallas_call(
        kern,
        out_shape=jax.ShapeDtypeStruct((m, d), F32),
        grid=(m // tm, nt),
        in_specs=[pl.BlockSpec((tm, d), lambda i, j: (i, 0)),
                  vec,
                  pl.BlockSpec((d, tn), lambda i, j: (0, j)),
                  pl.BlockSpec((d, tn), lambda i, j: (0, j + nt)),
                  pl.BlockSpec((FFN_CONV, tn), lambda i, j: (0, j)),
                  pl.BlockSpec((FFN_CONV, tn), lambda i, j: (0, j + nt)),
                  pl.BlockSpec((tn, d), lambda i, j: (j, 0)),
                  vec],
        out_specs=pl.BlockSpec((tm, d), lambda i, j: (i, 0)),
        scratch_shapes=[pltpu.VMEM((tm, d), BF16),
                        pltpu.VMEM((nt, SUBLANE, tn), F32),
                        pltpu.VMEM((nt, SUBLANE, tn), F32)],
        compiler_params=_params("arbitrary", "arbitrary"),
        name="ffn",
    )(h, nw.reshape(1, d), w_up, w_up, w_conv, w_conv, w_down, fw)


def _prep_ffn(w_up, w_conv, w_down):
    f = w_down.shape[0]
    fp = -(-f // FFN_TN) * FFN_TN
    pad = fp - f
    up = jnp.concatenate([jnp.pad(w_up[:, :f], ((0, 0), (0, pad))),
                          jnp.pad(w_up[:, f:], ((0, 0), (0, pad)))], axis=1).astype(BF16)
    cv = jnp.concatenate([jnp.pad(w_conv[:, :f], ((0, 0), (0, pad))),
                          jnp.pad(w_conv[:, f:], ((0, 0), (0, pad)))], axis=1)
    down = jnp.pad(w_down, ((0, pad), (0, 0))).astype(BF16)
    return up, cv, down


def _rwkv_proj_kernel(*refs, vres, tiles_per_seq):
    refs = list(refs)
    h_ref, nw_ref, mu_ref, wr_ref, wk_ref, wv_ref, w1_ref, a1_ref, g1_ref = refs[:9]
    w2_ref, a2_ref, g2_ref, vec_ref = refs[9:13]
    pos = 13
    if vres:
        v1_ref, v2_ref, vf_ref = refs[pos:pos + 3]
        pos += 3
    r_ref, k_ref, v_ref, ld_ref, kk_ref, a_ref, g_ref = refs[pos:pos + 7]
    pos += 7
    xm_ref, hw_ref, ha_ref, hg_ref = refs[pos:pos + 4]
    pos += 4
    if vres:
        hv_ref = refs[pos]
        pos += 1
    carry_ref = refs[pos]

    i = pl.program_id(0)

    @pl.when(pl.program_id(1) == 0)
    def _():
        tm, d = h_ref.shape
        rb = min(RW_PRE_ROWS, tm)
        fresh = (i % tiles_per_seq) == 0
        prev_last = jnp.where(fresh, 0.0, carry_ref[SUBLANE - 1:SUBLANE, :])
        row = lax.broadcasted_iota(jnp.int32, (rb, d), 0)
        nw = nw_ref[...]
        mu = mu_ref[...]
        for b in range(tm // rb):
            rows = slice(b * rb, (b + 1) * rb)
            xn = _rms(h_ref[rows, :], nw)
            xx = jnp.where(row == 0, prev_last, pltpu.roll(xn, 1, axis=0)) - xn
            prev_last = xn[rb - 1:rb, :]
            for q in range(6):
                xm_ref[q, rows, :] = (xn + xx * mu[q:q + 1, :]).astype(BF16)
        carry_ref[...] = xn[rb - SUBLANE:, :]
        hw_ref[...] = jnp.tanh(_dot(xm_ref[3], w1_ref[...])).astype(BF16)
        ha_ref[...] = _dot(xm_ref[4], a1_ref[...]).astype(BF16)
        hg_ref[...] = jax.nn.sigmoid(_dot(xm_ref[5], g1_ref[...])).astype(BF16)
        if vres:
            hv_ref[...] = _dot(xm_ref[2], v1_ref[...]).astype(BF16)

    vec = vec_ref[...]
    w0, a0, k_k, k_a, v0 = (vec[q:q + 1, :] for q in range(5))
    r = _dot(xm_ref[0], wr_ref[...])
    k = _dot(xm_ref[1], wk_ref[...])
    v = _dot(xm_ref[2], wv_ref[...])
    ld = -jax.nn.sigmoid(w0 + _dot(hw_ref[...], w2_ref[...])) * EXP_NEG_HALF
    a = jax.nn.sigmoid(a0 + _dot(ha_ref[...], a2_ref[...]))
    g = _dot(hg_ref[...], g2_ref[...])
    if vres:
        v = v + (vf_ref[...] - v) * jax.nn.sigmoid(v0 + _dot(hv_ref[...], v2_ref[...]))
    kkr = k * k_k
    ss = _split_dot(kkr * kkr, _head_ones(kkr.shape[1], RW_HEAD))
    r_ref[...] = r
    k_ref[...] = k * (1.0 + (a - 1.0) * k_a)
    v_ref[...] = v
    ld_ref[...] = ld
    kk_ref[...] = kkr / jnp.maximum(jnp.sqrt(ss), 1e-12)
    a_ref[...] = a
    g_ref[...] = g.astype(g_ref.dtype)


def _pad_cols(w, n):
    return jnp.pad(w, ((0, 0), (0, n - w.shape[1])))


def _pad_rows(w, n):
    return jnp.pad(w, ((0, n - w.shape[0]), (0, 0)))


def rwkv_proj(h, nw, p, v_first, seq):
    m, d = h.shape
    tm = min(RW_ROW_TILE, seq)
    tn = min(RW_COL_TILE, d)
    vres = v_first is not None
    lw, la, lg = (-(-p[n].shape[1] // LANE) * LANE for n in ("w1", "a1", "g1"))
    full = lambda shape: pl.BlockSpec(shape, lambda i, j: (0,) * len(shape))
    col = lambda rows: pl.BlockSpec((rows, tn), lambda i, j: (0, j))
    tile = pl.BlockSpec((tm, tn), lambda i, j: (i, j))
    rkv = lambda q: pl.BlockSpec((None, d, tn), lambda i, j: (q, 0, j))
    zeros = jnp.zeros((d,), F32)
    vec = jnp.stack([p["w0"], p["a0"], p["k_k"], p["k_a"], p["v0"] if vres else zeros,
                     zeros, zeros, zeros])
    mu = jnp.pad(p["mu"], ((0, SUBLANE - p["mu"].shape[0]), (0, 0)))
    args = [h, nw.reshape(1, d), mu, p["w_rkv"], p["w_rkv"], p["w_rkv"],
            _pad_cols(p["w1"], lw), _pad_cols(p["a1"], la), _pad_cols(p["g1"], lg),
            _pad_rows(p["w2"], lw), _pad_rows(p["a2"], la), _pad_rows(p["g2"], lg), vec]
    in_specs = [pl.BlockSpec((tm, d), lambda i, j: (i, 0)), full((1, d)), full((SUBLANE, d)),
                rkv(0), rkv(1), rkv(2), full((d, lw)), full((d, la)), full((d, lg)),
                col(lw), col(la), col(lg), col(SUBLANE)]
    scratch = [pltpu.VMEM((6, tm, d), BF16), pltpu.VMEM((tm, lw), BF16),
               pltpu.VMEM((tm, la), BF16), pltpu.VMEM((tm, lg), BF16)]
    if vres:
        lv = -(-p["v1"].shape[1] // LANE) * LANE
        args += [_pad_cols(p["v1"], lv), _pad_rows(p["v2"], lv), v_first]
        in_specs += [full((d, lv)), col(lv), tile]
        scratch.append(pltpu.VMEM((tm, lv), BF16))
    scratch.append(pltpu.VMEM((SUBLANE, d), F32))
    out = jax.ShapeDtypeStruct((m, d), F32)
    kern = functools.partial(_rwkv_proj_kernel, vres=vres, tiles_per_seq=seq // tm)
    return pl.pallas_call(
        kern,
        out_shape=[out] * 6 + [jax.ShapeDtypeStruct((m, d), BF16)],
        grid=(m // tm, d // tn),
        in_specs=in_specs,
        out_specs=[tile] * 7,
        scratch_shapes=scratch,
        compiler_params=_params("arbitrary", "arbitrary"),
        name="rwkv_proj",
    )(*args)


def _rwkv_group(r, ld, k, v, kk, a, z, masks):
    tri, bdm, strict, incl, eye = masks
    C, W = r[0].shape
    G = W // RW_HEAD
    bfl = lambda xs: [x.astype(BF16) for x in xs]

    def each(f, *ls):
        return [f(*xs) for xs in zip(*ls)]

    def bd(xs):
        return [jnp.where(bdm, jnp.concatenate([x.astype(BF16)] * G, axis=0), 0) for x in xs]

    cum = each(lambda x: _split_dot_left(tri, x), ld)
    last = [c[C - 1:C, :] for c in cum]
    b = each(jnp.multiply, kk, a)
    einv = [jnp.exp(-c) for c in cum]
    etail = each(lambda l, c: jnp.exp(l - c), last, cum)
    at = each(lambda x, c, l: -x * jnp.exp(c - l), kk, cum, ld)
    rt = each(lambda x, c: x * jnp.exp(c), r, cum)
    bt = each(jnp.multiply, b, einv)
    kt = each(jnp.multiply, k, einv)
    bh = each(jnp.multiply, b, etail)
    kh = each(jnp.multiply, k, etail)

    vb = bd(v)
    lhs = each(lambda x, y: jnp.concatenate([x, y], axis=0).astype(BF16), at, rt)
    rhs = each(lambda x, y: jnp.concatenate([x, y], axis=0), bd(bt), bd(kt))
    gram = each(_dot_nt, lhs, rhs)
    a_ab = [jnp.where(strict, x[:C, :W], 0.0) for x in gram]
    a_ak = [jnp.where(strict, x[:C, W:], 0.0).astype(BF16) for x in gram]
    a_rb = [jnp.where(incl, x[C:, :W], 0.0).astype(BF16) for x in gram]
    a_rk = [jnp.where(incl, x[C:, W:], 0.0).astype(BF16) for x in gram]

    p = a_ab
    t = [eye + x for x in a_ab]
    pb = bd(p)
    for _ in range(int(math.log2(C)) - 1):
        p = each(_dot, bfl(p), pb)
        pb = bd(p)
        t = each(lambda x, y: x + _dot(x.astype(BF16), y), t, pb)
    tb = bfl(t)

    wa = each(_dot, tb, bd(at))
    x1 = each(_dot, a_ak, vb)
    uv = each(_dot, tb, bd(x1))
    zb = bfl(z)
    u = each(lambda w, s, x: _dot_nt(w.astype(BF16), s) + x, wa, zb, uv)
    y = each(lambda q, s, x, ub, w, xb: _dot_nt(q.astype(BF16), s) + _dot(x, ub) + _dot(w, xb),
             rt, zb, a_rb, bd(u), a_rk, vb)
    gm = each(lambda x, w, s, q: jnp.where(
        bdm, _dot_tn(jnp.concatenate([x, w], axis=0).astype(BF16),
                     jnp.concatenate([s, q], axis=0).astype(BF16)), 0.0), u, v, bh, kh)
    zn = each(lambda s, l, g_: s * jnp.exp(l) + g_, z, last, gm)
    return y, zn


def _split_dot_left(w, x):
    hi = x.astype(BF16)
    lo = (x - hi.astype(F32)).astype(BF16)
    return _dot(w, hi) + _dot(w, lo)


def _rwkv_masks(C, W):
    trow = lax.broadcasted_iota(jnp.int32, (C, C), 0)
    tcol = lax.broadcasted_iota(jnp.int32, (C, C), 1)
    tri = (trow >= tcol).astype(BF16)
    brow = lax.broadcasted_iota(jnp.int32, (W, W), 0)
    bcol = lax.broadcasted_iota(jnp.int32, (W, W), 1)
    bdm = (brow // C) == (bcol // RW_HEAD)
    lrow = lax.broadcasted_iota(jnp.int32, (C, W), 0)
    lcol = lax.broadcasted_iota(jnp.int32, (C, W), 1) % RW_HEAD
    eye = jnp.where(lrow == lcol, 1.0, 0.0)
    return tri, bdm, lrow > lcol, lrow >= lcol, eye


def _rwkv_chunk_kernel(r_ref, ld_ref, k_ref, v_ref, kk_ref, a_ref, g_ref, vec_ref, o_ref, z_ref):
    @pl.when(pl.program_id(2) == 0)
    def _():
        z_ref[...] = jnp.zeros_like(z_ref)

    C = r_ref.shape[0]
    W = RW_GROUP * RW_HEAD
    masks = _rwkv_masks(C, W)
    hsum = _head_ones(W, RW_HEAD)
    inv_n = 1.0 / RW_HEAD
    ng = r_ref.shape[1] // W
    sls = [slice(gi * W, (gi + 1) * W) for gi in range(ng)]
    take = lambda ref: [ref[:, sl] for sl in sls]
    r, k, v = take(r_ref), take(k_ref), take(v_ref)
    ys, zs = _rwkv_group(r, take(ld_ref), k, v, take(kk_ref), take(a_ref),
                         [z_ref[gi] for gi in range(ng)], masks)
    for gi in range(ng):
        z_ref[gi] = zs[gi]
    mean = [_dot(y.astype(BF16), hsum) * inv_n for y in ys]
    yc = [y - mu for y, mu in zip(ys, mean)]
    var = [_dot((x * x).astype(BF16), hsum) * inv_n for x in yc]
    for gi, sl in enumerate(sls):
        vec = vec_ref[:, sl]
        r_k, lnx_w, lnx_b = vec[0:1, :], vec[1:2, :], vec[2:3, :]
        yn = yc[gi] * lax.rsqrt(var[gi] + RW_GN_EPS) * lnx_w + lnx_b
        bonus = _dot((r[gi] * k[gi] * r_k).astype(BF16), hsum) * v[gi]
        o_ref[:, sl] = ((yn + bonus) * g_ref[:, sl].astype(F32)).astype(o_ref.dtype)


def rwkv_scan(r, ld, k, v, kk, a, g, r_k, lnx_w, lnx_b, batch, seq):
    m, d = r.shape
    C = RW_CHUNK
    W = RW_GROUP * RW_HEAD
    assert RW_GROUP * C == W
    wb = min(RW_STEP_GROUPS * W, d)
    nc = seq // C
    zeros = jnp.zeros((d,), F32)
    vec = jnp.stack([r_k.reshape(d), lnx_w, lnx_b] + [zeros] * (SUBLANE - 3))
    spec = pl.BlockSpec((C, wb), lambda b, g, c: (b * nc + c, g))
    return pl.pallas_call(
        _rwkv_chunk_kernel,
        out_shape=jax.ShapeDtypeStruct((m, d), BF16),
        grid=(batch, d // wb, nc),
        in_specs=[spec] * 7 + [pl.BlockSpec((SUBLANE, wb), lambda b, g, c: (0, g))],
        out_specs=spec,
        scratch_shapes=[pltpu.VMEM((wb // W, W, W), F32)],
        compiler_params=_params("parallel", "parallel", "arbitrary"),
        name="rwkv_scan",
    )(r, ld, k, v, kk, a, g, vec)


def _retention_kernel(q_ref, k_ref, v_ref, g_ref, cos_ref, sin_ref, intra_ref, cross_ref,
                      tail_ref, gnw_ref, o_ref, st_ref):
    c = pl.program_id(2)

    @pl.when(c == 0)
    def _():
        st_ref[...] = jnp.zeros_like(st_ref)

    C = q_ref.shape[0]
    half = RET_QK_HEAD // 2
    cos = cos_ref[...]
    sin = sin_ref[...]

    def rope(t):
        t1 = t[:, :half]
        t2 = t[:, half:]
        return jnp.concatenate([t1 * cos - t2 * sin, t1 * sin + t2 * cos], axis=-1)

    hs = range(q_ref.shape[1] // RET_QK_HEAD)
    qk = lambda ref, h: ref[:, h * RET_QK_HEAD:(h + 1) * RET_QK_HEAD].astype(F32)
    vsl = lambda h: slice(h * RET_V_HEAD, (h + 1) * RET_V_HEAD)
    qb = [rope(qk(q_ref, h)).astype(BF16) for h in hs]
    k = [rope(qk(k_ref, h)) * (RET_QK_HEAD ** -0.5) for h in hs]
    vb = [v_ref[:, vsl(h)].astype(BF16) for h in hs]
    cross = [cross_ref[h] for h in hs]
    scores = [(_dot_nt(qb[h], k[h].astype(BF16)) * intra_ref[h]).astype(BF16) for h in hs]
    st = [st_ref[h] for h in hs]
    o = [_dot(scores[h], vb[h]) + _dot(qb[h], st[h].astype(BF16)) * cross[h] for h in hs]
    for h in hs:
        st_ref[h] = st[h] * cross[h][C - 1:C, :] + _dot_tn((k[h] * tail_ref[h]).astype(BF16), vb[h])
    for h in hs:
        mu = jnp.mean(o[h], axis=-1, keepdims=True)
        oc = o[h] - mu
        var = jnp.mean(oc * oc, axis=-1, keepdims=True)
        on = oc * lax.rsqrt(var + GN_EPS) * gnw_ref[:, vsl(h)]
        g = g_ref[:, vsl(h)].astype(F32)
        o_ref[:, vsl(h)] = (g * jax.nn.sigmoid(g) * on).astype(o_ref.dtype)


def retention(proj, gn_w, batch, seq):
    m, n = proj.shape
    heads = n // (2 * RET_QK_HEAD + 2 * RET_V_HEAD)
    C = RET_CHUNK
    nc = seq // C
    half = RET_QK_HEAD // 2
    pos = jnp.arange(seq, dtype=F32)
    inv_freq = 1.0 / (ROPE_BASE ** (jnp.arange(0, RET_QK_HEAD, 2, dtype=F32) / RET_QK_HEAD))
    ang = pos[:, None] * inv_freq[None, :]
    cos, sin = jnp.cos(ang), jnp.sin(ang)
    log_gamma = jnp.log1p(-jnp.exp2(-5.0 - jnp.arange(heads, dtype=F32)))
    cp = jnp.arange(C, dtype=F32)
    rel = cp[:, None] - cp[None, :]
    intra = jnp.where(rel >= 0, jnp.exp(log_gamma[:, None, None] * jnp.maximum(rel, 0.0)), 0.0)
    cross = jnp.exp(log_gamma[:, None] * (cp + 1.0))[:, :, None]
    tail = jnp.exp(log_gamma[:, None] * (C - 1.0 - cp))[:, :, None]
    vdim = heads * RET_V_HEAD
    nh = min(RET_STEP_HEADS, heads)
    nb = heads // nh
    qw, vw = nh * RET_QK_HEAD, nh * RET_V_HEAD
    return pl.pallas_call(
        _retention_kernel,
        out_shape=jax.ShapeDtypeStruct((m, vdim), BF16),
        grid=(batch, nb, nc),
        in_specs=[pl.BlockSpec((C, qw), lambda b, h, c: (b * nc + c, h)),
                  pl.BlockSpec((C, qw), lambda b, h, c: (b * nc + c, nb + h)),
                  pl.BlockSpec((C, vw), lambda b, h, c: (b * nc + c, nb + h)),
                  pl.BlockSpec((C, vw), lambda b, h, c: (b * nc + c, 2 * nb + h)),
                  pl.BlockSpec((C, half), lambda b, h, c: (c, 0)),
                  pl.BlockSpec((C, half), lambda b, h, c: (c, 0)),
                  pl.BlockSpec((nh, C, C), lambda b, h, c: (h, 0, 0)),
                  pl.BlockSpec((nh, C, 1), lambda b, h, c: (h, 0, 0)),
                  pl.BlockSpec((nh, C, 1), lambda b, h, c: (h, 0, 0)),
                  pl.BlockSpec((1, vw), lambda b, h, c: (0, h))],
        out_specs=pl.BlockSpec((C, vw), lambda b, h, c: (b * nc + c, h)),
        scratch_shapes=[pltpu.VMEM((nh, RET_QK_HEAD, RET_V_HEAD), F32)],
        compiler_params=_params("parallel", "parallel", "arbitrary"),
        name="retention",
    )(proj, proj, proj, proj, cos, sin, intra, cross, tail, gn_w.reshape(1, vdim))


def _hgrn_kernel(q_ref, f_ref, i_ref, g_ref, lb_ref, nw_ref, o_ref, zt_ref, cum_ref, k_ref):
    @pl.when(pl.program_id(2) == 0)
    def _():
        zt_ref[...] = jnp.zeros_like(zt_ref)

    C, wd = q_ref.shape
    nh = wd // HG_HEAD
    B = HG_SUB
    H = B // 2
    lb = lb_ref[...]
    qr = q_ref[...]
    q = qr * jax.nn.sigmoid(qr)
    sig = jax.nn.sigmoid(f_ref[...])
    lf = jnp.log(lb + (1.0 - lb) * sig)
    k_ref[...] = (1.0 - lb) * (1.0 - sig)

    trow = lax.broadcasted_iota(jnp.int32, (C, C), 0)
    tcol = lax.broadcasted_iota(jnp.int32, (C, C), 1)
    tri = ((trow >= tcol) & ((trow // B) == (tcol // B))).astype(BF16)
    cum_ref[...] = _split_dot_left(tri, lf) * LOG2_E

    n_row = lax.broadcasted_iota(jnp.int32, (H, wd), 0)
    ones = jnp.ones((HG_HEAD, HG_HEAD), BF16)
    gate = nw_ref[...] * jax.nn.sigmoid(g_ref[...])

    def row(ref, r, cols=slice(None)):
        x = ref[r:r + 1, cols]
        return jnp.broadcast_to(x, (H, x.shape[1]))

    def pairs(qn, cn, m0, masked):
        out = []
        for mm in range(H):
            d = cn - row(cum_ref, m0 + mm)
            if masked and mm > 0:
                d = jnp.where(n_row >= mm, d, -jnp.inf)
            out.append(qn * row(k_ref, m0 + mm) * jnp.exp2(d))
        return out

    zts = [zt_ref[hh] for hh in range(nh)]
    for s in range(C // B):
        r0 = s * B
        rows = slice(r0, r0 + B)
        cs = cum_ref[rows, :]
        last = cs[B - 1:B, :]
        qs = q[rows]
        qe = (qs * jnp.exp2(cs)).astype(BF16)
        kd = (k_ref[rows, :] * jnp.exp2(last - cs)).astype(BF16)
        dec = jnp.exp2(last)
        p = jnp.concatenate(pairs(qs[:H], cs[:H], r0, True) + pairs(qs[H:], cs[H:], r0, False)
                            + pairs(qs[H:], cs[H:], r0 + H, True), axis=0).astype(BF16)
        vs_b = i_ref[rows, :].astype(BF16)
        outs = []
        for hh in range(nh):
            ls = slice(hh * HG_HEAD, (hh + 1) * HG_HEAD)
            zt = zts[hh]
            o = _dot_nt(qe[:, ls], zt.astype(BF16))
            srep = _dot(p[:, ls], ones)
            blk = lambda part, mm: srep[(part * H + mm) * H:(part * H + mm + 1) * H, :]
            o_lo = sum(blk(0, mm) * row(i_ref, r0 + mm, ls) for mm in range(H))
            o_hi = sum(blk(1, mm) * row(i_ref, r0 + mm, ls) + blk(2, mm) * row(i_ref, r0 + H + mm, ls)
                       for mm in range(H))
            o = o + jnp.concatenate([o_lo, o_hi], axis=0)
            zts[hh] = zt * dec[:, ls] + _dot_tn(vs_b[:, ls], kd[:, ls])
            outs.append(o * lax.rsqrt(jnp.mean(o * o, axis=-1, keepdims=True) + RMS_EPS))
        o_ref[rows, :] = (jnp.concatenate(outs, axis=-1) * gate[rows]).astype(o_ref.dtype)
    for hh in range(nh):
        zt_ref[hh] = zts[hh]


def hgrn_mix(proj, lb, norm_w, batch, seq):
    m, n = proj.shape
    d = n // 4
    wb = min(HG_GROUP * HG_HEAD, d)
    nb = d // wb
    C = HG_CHUNK
    nc = seq // C

    def spec(off):
        return pl.BlockSpec((C, wb), lambda b, h, c: (b * nc + c, off * nb + h))

    vec = pl.BlockSpec((1, wb), lambda b, h, c: (0, h))
    return pl.pallas_call(
        _hgrn_kernel,
        out_shape=jax.ShapeDtypeStruct((m, d), BF16),
        grid=(batch, nb, nc),
        in_specs=[spec(0), spec(1), spec(2), spec(3), vec, vec],
        out_specs=spec(0),
        scratch_shapes=[pltpu.VMEM((wb // HG_HEAD, HG_HEAD, HG_HEAD), F32),
                        pltpu.VMEM((C, wb), F32), pltpu.VMEM((C, wb), F32)],
        compiler_params=_params("parallel", "parallel", "arbitrary"),
        name="hgrn",
    )(proj, proj, proj, proj, lb.reshape(1, d), norm_w.reshape(1, d))


def kernel(x, norm_mix, norm_ffn, norm_final, ffn_w_up, ffn_conv, ffn_w_down, rw_mu, rw_w_rkv, rw_w0, rw_w1, rw_w2, rw_a0, rw_a1, rw_a2, rw_v0, rw_v1, rw_v2, rw_g1, rw_g2, rw_k_k, rw_k_a, rw_r_k, rw_lnx_w, rw_lnx_b, rw_w_o, ret_w_in, ret_gn_w, ret_w_o, hg_w_in, hg_lb_logits, hg_norm_w, hg_w_o):
    batch, seq, d = x.shape
    depth = norm_mix.shape[0]
    m = batch * seq
    lb_all = jnp.cumsum(jax.nn.softmax(hg_lb_logits.astype(F32), axis=0), axis=0)
    lb_all = lb_all - lb_all[0]
    bf = lambda t: t.astype(BF16)

    h = x.reshape(m, d)
    v_first = None
    for layer in range(depth):
        kind = layer % 3
        j = layer // 3
        if kind == 0:
            p = dict(mu=rw_mu[j], w_rkv=bf(rw_w_rkv[j]), w0=rw_w0[j], w1=bf(rw_w1[j]), w2=bf(rw_w2[j]),
                     a0=rw_a0[j], a1=bf(rw_a1[j]), a2=bf(rw_a2[j]), g1=bf(rw_g1[j]), g2=bf(rw_g2[j]),
                     k_k=rw_k_k[j], k_a=rw_k_a[j])
            if j > 0:
                p.update(v0=rw_v0[j - 1], v1=bf(rw_v1[j - 1]), v2=bf(rw_v2[j - 1]))
            r, k, v, ld, kk, a, g = rwkv_proj(h, norm_mix[layer], p, v_first, seq)
            if v_first is None:
                v_first = v
            mixed = rwkv_scan(r, ld, k, v, kk, a, g, rw_r_k[j], rw_lnx_w[j], rw_lnx_b[j], batch, seq)
            h = matmul_res(mixed, bf(rw_w_o[j]), h)
        elif kind == 1:
            proj = norm_matmul(h, norm_mix[layer], bf(ret_w_in[j]), BF16)
            gated = retention(proj, ret_gn_w[j], batch, seq)
            h = matmul_res(gated, bf(ret_w_o[j]), h)
        else:
            proj = norm_matmul(h, norm_mix[layer], bf(hg_w_in[j]))
            gated = hgrn_mix(proj, lb_all[layer], hg_norm_w[j], batch, seq)
            h = matmul_res(gated, bf(hg_w_o[j]), h)
        up, cv, down = _prep_ffn(ffn_w_up[layer], ffn_conv[layer], ffn_w_down[layer])
        final_w = norm_final if layer == depth - 1 else None
        h = ffn_block(h, norm_ffn[layer], up, cv, down, seq, final_w)
    return h.reshape(batch, seq, d)
```

```python
import functools
import math

import jax
import jax.numpy as jnp
from jax import lax
from jax.experimental import pallas as pl
from jax.experimental.pallas import tpu as pltpu

F32 = jnp.float32
BF16 = jnp.bfloat16

RMS_EPS = 1e-6
GN_EPS = 1e-5
ROPE_BASE = 10000.0

RW_HEAD = 64
RW_GN_EPS = RW_HEAD * 1e-5
RW_CHUNK = 64
RW_GROUP = 4
RW_STEP_GROUPS = 8
RW_ROW_TILE = 512
RW_PRE_ROWS = 128
RW_COL_TILE = 256
EXP_NEG_HALF = math.exp(-0.5)
LOG2_E = 1.0 / math.log(2.0)

RET_QK_HEAD = 256
RET_V_HEAD = 512
RET_CHUNK = 128
RET_STEP_HEADS = 4

HG_HEAD = 128
HG_CHUNK = 64
HG_GROUP = 8

FFN_CONV = 3
LANE = 128
SUBLANE = 8
ROW_TILE = 512
MM_ROW_TILE = 1024
MM_TILE_ELEMS = 2 * 1024 * 1024
NORM_ROWS = 256
FFN_TN = 512

VMEM_LIMIT = 56 * 1024 * 1024


def _dot(a, b):
    return jnp.dot(a, b, preferred_element_type=F32)


def _dot_nt(a, b):
    return lax.dot_general(a, b, (((1,), (1,)), ((), ())), preferred_element_type=F32)


def _dot_tn(a, b):
    return lax.dot_general(a, b, (((0,), (0,)), ((), ())), preferred_element_type=F32)


def _params(*sem):
    return pltpu.CompilerParams(dimension_semantics=sem, vmem_limit_bytes=VMEM_LIMIT)


def _rms(x, w):
    ms = jnp.mean(x * x, axis=-1, keepdims=True)
    return x * lax.rsqrt(ms + RMS_EPS) * w


def _head_ones(n, head):
    row = lax.broadcasted_iota(jnp.int32, (n, n), 0)
    col = lax.broadcasted_iota(jnp.int32, (n, n), 1)
    return ((row // head) == (col // head)).astype(BF16)


def _split_dot(x, w):
    hi = x.astype(BF16)
    lo = (x - hi.astype(F32)).astype(BF16)
    return _dot(hi, w) + _dot(lo, w)


def _shift_rows(u, carry, fresh):
    c = jnp.where(fresh, 0.0, carry)
    row = lax.broadcasted_iota(jnp.int32, (SUBLANE, u.shape[1]), 0)
    r1 = pltpu.roll(u, 1, axis=0)
    r2 = pltpu.roll(u, 2, axis=0)
    top1 = jnp.where(row == 0, c[7:8, :], r1[:SUBLANE])
    top2 = jnp.where(row == 0, c[6:7, :], jnp.where(row == 1, c[7:8, :], r2[:SUBLANE]))
    return (jnp.concatenate([top1, r1[SUBLANE:]], axis=0),
            jnp.concatenate([top2, r2[SUBLANE:]], axis=0))


def _mm_res_kernel(x_ref, w_ref, r_ref, o_ref):
    o_ref[...] = (r_ref[...] + _dot(x_ref[...], w_ref[...])).astype(o_ref.dtype)


def _col_tile(n):
    for t in (1024, 512, 256, 128):
        if n % t == 0:
            return t
    return n


def matmul_res(x, w, res):
    m, k = x.shape
    n = w.shape[1]
    tm = min(MM_ROW_TILE, m)
    tn = min(_col_tile(n), MM_TILE_ELEMS // k)
    return pl.pallas_call(
        _mm_res_kernel,
        out_shape=jax.ShapeDtypeStruct((m, n), F32),
        grid=(m // tm, n // tn),
        in_specs=[pl.BlockSpec((tm, k), lambda i, j: (i, 0)),
                  pl.BlockSpec((k, tn), lambda i, j: (0, j)),
                  pl.BlockSpec((tm, tn), lambda i, j: (i, j))],
        out_specs=pl.BlockSpec((tm, tn), lambda i, j: (i, j)),
        compiler_params=_params("parallel", "parallel"),
        name="matmul_res",
    )(x, w, res)


def _norm_mm_kernel(h_ref, nw_ref, w_ref, o_ref, xs_ref):
    @pl.when(pl.program_id(1) == 0)
    def _():
        tm = h_ref.shape[0]
        rb = min(NORM_ROWS, tm)
        nw = nw_ref[...]
        for r0 in range(0, tm, rb):
            xs_ref[r0:r0 + rb, :] = _rms(h_ref[r0:r0 + rb, :], nw).astype(BF16)

    o_ref[...] = _dot(xs_ref[...], w_ref[...]).astype(o_ref.dtype)


def norm_matmul(h, nw, w, out_dtype=F32):
    m, d = h.shape
    n = w.shape[1]
    tm = min(MM_ROW_TILE, m)
    tn = _col_tile(n)
    return pl.pallas_call(
        _norm_mm_kernel,
        out_shape=jax.ShapeDtypeStruct((m, n), out_dtype),
        grid=(m // tm, n // tn),
        in_specs=[pl.BlockSpec((tm, d), lambda i, j: (i, 0)),
                  pl.BlockSpec((1, d), lambda i, j: (0, 0)),
                  pl.BlockSpec((d, tn), lambda i, j: (0, j))],
        out_specs=pl.BlockSpec((tm, tn), lambda i, j: (i, j)),
        scratch_shapes=[pltpu.VMEM((tm, d), BF16)],
        compiler_params=_params("parallel", "arbitrary"),
        name="norm_matmul",
    )(h, nw.reshape(1, d), w)


def _ffn_kernel(h_ref, nw_ref, wg_ref, wv_ref, cg_ref, cv_ref, wd_ref, fw_ref, o_ref,
                xs_ref, carry_g_ref, carry_v_ref, *, tiles_per_seq, final_norm):
    i = pl.program_id(0)
    j = pl.program_id(1)
    tm = xs_ref.shape[0]

    @pl.when(j == 0)
    def _():
        rb = min(NORM_ROWS, tm)
        nw = nw_ref[...]
        for r0 in range(0, tm, rb):
            h = h_ref[r0:r0 + rb, :]
            xs_ref[r0:r0 + rb, :] = _rms(h, nw).astype(BF16)
            o_ref[r0:r0 + rb, :] = h

    fresh = (i % tiles_per_seq) == 0
    xs = xs_ref[...]

    def conv(w_ref, c_ref, carry_ref):
        u = _dot(xs, w_ref[...])
        u1, u2 = _shift_rows(u, carry_ref[j], fresh)
        carry_ref[j] = u[tm - SUBLANE:, :]
        cw = c_ref[...]
        return u2 * cw[0:1, :] + u1 * cw[1:2, :] + u * cw[2:3, :]

    gate = conv(wg_ref, cg_ref, carry_g_ref)
    val = conv(wv_ref, cv_ref, carry_v_ref)
    act = (gate * jax.nn.sigmoid(gate) * val).astype(BF16)
    o_ref[...] += _dot(act, wd_ref[...])

    if final_norm:
        @pl.when(j == pl.num_programs(1) - 1)
        def _():
            o_ref[...] = _rms(o_ref[...], fw_ref[...])


def ffn_block(h, nw, w_gate, w_val, c_gate, c_val, w_down, seq, final_w=None):
    m, d = h.shape
    f = w_down.shape[0]
    tm = min(ROW_TILE, seq)
    tn = FFN_TN
    nt = f // tn
    kern = functools.partial(_ffn_kernel, tiles_per_seq=seq // tm, final_norm=final_w is not None)
    fw = (nw if final_w is None else final_w).reshape(1, d)
    vec = pl.BlockSpec((1, d), lambda i, j: (0, 0))
    col = lambda rows: pl.BlockSpec((rows, tn), lambda i, j: (0, j))
    return pl.pallas_call(
        kern,
        out_shape=jax.ShapeDtypeStruct((m, d), F32),
        grid=(m // tm, nt),
        in_specs=[pl.BlockSpec((tm, d), lambda i, j: (i, 0)), vec, col(d), col(d),
                  col(FFN_CONV), col(FFN_CONV), pl.BlockSpec((tn, d), lambda i, j: (j, 0)), vec],
        out_specs=pl.BlockSpec((tm, d), lambda i, j: (i, 0)),
        scratch_shapes=[pltpu.VMEM((tm, d), BF16),
                        pltpu.VMEM((nt, SUBLANE, tn), F32),
                        pltpu.VMEM((nt, SUBLANE, tn), F32)],
        compiler_params=_params("arbitrary", "arbitrary"),
        name="ffn",
    )(h, nw.reshape(1, d), w_gate, w_val, c_gate, c_val, w_down, fw)


def _prep_ffn(w_up, w_conv, w_down):
    f = w_down.shape[0]
    pad = -f % FFN_TN
    cols = lambda w: jnp.pad(w, ((0, 0), (0, pad)))
    return (cols(w_up[:, :f]).astype(BF16), cols(w_up[:, f:]).astype(BF16),
            cols(w_conv[:, :f]), cols(w_conv[:, f:]),
            jnp.pad(w_down, ((0, pad), (0, 0))).astype(BF16))


def _rwkv_proj_kernel(*refs, vres, tiles_per_seq):
    refs = list(refs)
    h_ref, nw_ref, mu_ref, wr_ref, wk_ref, wv_ref, w1_ref, a1_ref, g1_ref = refs[:9]
    w2_ref, a2_ref, g2_ref, vec_ref = refs[9:13]
    pos = 13
    if vres:
        v1_ref, v2_ref, vf_ref = refs[pos:pos + 3]
        pos += 3
    r_ref, k_ref, v_ref, ld_ref, kk_ref, a_ref, g_ref = refs[pos:pos + 7]
    pos += 7
    xm_ref, hw_ref, ha_ref, hg_ref = refs[pos:pos + 4]
    pos += 4
    if vres:
        hv_ref = refs[pos]
        pos += 1
    carry_ref = refs[pos]

    i = pl.program_id(0)

    @pl.when(pl.program_id(1) == 0)
    def _():
        tm, d = h_ref.shape
        rb = min(RW_PRE_ROWS, tm)
        fresh = (i % tiles_per_seq) == 0
        prev_last = jnp.where(fresh, 0.0, carry_ref[SUBLANE - 1:SUBLANE, :])
        row = lax.broadcasted_iota(jnp.int32, (rb, d), 0)
        nw = nw_ref[...]
        mu = mu_ref[...]
        for b in range(tm // rb):
            rows = slice(b * rb, (b + 1) * rb)
            xn = _rms(h_ref[rows, :], nw)
            xx = jnp.where(row == 0, prev_last, pltpu.roll(xn, 1, axis=0)) - xn
            prev_last = xn[rb - 1:rb, :]
            for q in range(6):
                xm_ref[q, rows, :] = (xn + xx * mu[q:q + 1, :]).astype(BF16)
        carry_ref[...] = xn[rb - SUBLANE:, :]
        hw_ref[...] = jnp.tanh(_dot(xm_ref[3], w1_ref[...])).astype(BF16)
        ha_ref[...] = _dot(xm_ref[4], a1_ref[...]).astype(BF16)
        hg_ref[...] = jax.nn.sigmoid(_dot(xm_ref[5], g1_ref[...])).astype(BF16)
        if vres:
            hv_ref[...] = _dot(xm_ref[2], v1_ref[...]).astype(BF16)

    vec = vec_ref[...]
    w0, a0, k_k, k_a, v0 = (vec[q:q + 1, :] for q in range(5))
    r = _dot(xm_ref[0], wr_ref[...])
    k = _dot(xm_ref[1], wk_ref[...])
    v = _dot(xm_ref[2], wv_ref[...])
    ld = -jax.nn.sigmoid(w0 + _dot(hw_ref[...], w2_ref[...])) * EXP_NEG_HALF
    a = jax.nn.sigmoid(a0 + _dot(ha_ref[...], a2_ref[...]))
    g = _dot(hg_ref[...], g2_ref[...])
    if vres:
        v = v + (vf_ref[...] - v) * jax.nn.sigmoid(v0 + _dot(hv_ref[...], v2_ref[...]))
    kkr = k * k_k
    ss = _split_dot(kkr * kkr, _head_ones(kkr.shape[1], RW_HEAD))
    r_ref[...] = r.astype(r_ref.dtype)
    k_ref[...] = (k * (1.0 + (a - 1.0) * k_a)).astype(k_ref.dtype)
    v_ref[...] = v
    ld_ref[...] = ld
    kk_ref[...] = (kkr / jnp.maximum(jnp.sqrt(ss), 1e-12)).astype(kk_ref.dtype)
    a_ref[...] = a.astype(a_ref.dtype)
    g_ref[...] = g.astype(g_ref.dtype)


def _pad_cols(w, n):
    return jnp.pad(w, ((0, 0), (0, n - w.shape[1])))


def _pad_rows(w, n):
    return jnp.pad(w, ((0, n - w.shape[0]), (0, 0)))


def rwkv_proj(h, nw, p, v_first, seq):
    m, d = h.shape
    tm = min(RW_ROW_TILE, seq)
    tn = min(RW_COL_TILE, d)
    vres = v_first is not None
    lw, la, lg = (-(-p[n].shape[1] // LANE) * LANE for n in ("w1", "a1", "g1"))
    full = lambda shape: pl.BlockSpec(shape, lambda i, j: (0,) * len(shape))
    col = lambda rows: pl.BlockSpec((rows, tn), lambda i, j: (0, j))
    tile = pl.BlockSpec((tm, tn), lambda i, j: (i, j))
    rkv = lambda q: pl.BlockSpec((None, d, tn), lambda i, j: (q, 0, j))
    zeros = jnp.zeros((d,), F32)
    vec = jnp.stack([p["w0"], p["a0"], p["k_k"], p["k_a"], p["v0"] if vres else zeros,
                     zeros, zeros, zeros])
    mu = jnp.pad(p["mu"], ((0, SUBLANE - p["mu"].shape[0]), (0, 0)))
    args = [h, nw.reshape(1, d), mu, p["w_rkv"], p["w_rkv"], p["w_rkv"],
            _pad_cols(p["w1"], lw), _pad_cols(p["a1"], la), _pad_cols(p["g1"], lg),
            _pad_rows(p["w2"], lw), _pad_rows(p["a2"], la), _pad_rows(p["g2"], lg), vec]
    in_specs = [pl.BlockSpec((tm, d), lambda i, j: (i, 0)), full((1, d)), full((SUBLANE, d)),
                rkv(0), rkv(1), rkv(2), full((d, lw)), full((d, la)), full((d, lg)),
                col(lw), col(la), col(lg), col(SUBLANE)]
    scratch = [pltpu.VMEM((6, tm, d), BF16), pltpu.VMEM((tm, lw), BF16),
               pltpu.VMEM((tm, la), BF16), pltpu.VMEM((tm, lg), BF16)]
    if vres:
        lv = -(-p["v1"].shape[1] // LANE) * LANE
        args += [_pad_cols(p["v1"], lv), _pad_rows(p["v2"], lv), v_first]
        in_specs += [full((d, lv)), col(lv), tile]
        scratch.append(pltpu.VMEM((tm, lv), BF16))
    scratch.append(pltpu.VMEM((SUBLANE, d), F32))
    out = jax.ShapeDtypeStruct((m, d), F32)
    half = jax.ShapeDtypeStruct((m, d), BF16)
    kern = functools.partial(_rwkv_proj_kernel, vres=vres, tiles_per_seq=seq // tm)
    return pl.pallas_call(
        kern,
        out_shape=[half, half, out, out, half, half, half],
        grid=(m // tm, d // tn),
        in_specs=in_specs,
        out_specs=[tile] * 7,
        scratch_shapes=scratch,
        compiler_params=_params("arbitrary", "arbitrary"),
        name="rwkv_proj",
    )(*args)


def _rwkv_group(r, ld, k, v, kk, a, z, masks):
    tri, bdm, strict, incl, eye = masks
    C, W = r[0].shape
    G = W // RW_HEAD
    bfl = lambda xs: [x.astype(BF16) for x in xs]

    def each(f, *ls):
        return [f(*xs) for xs in zip(*ls)]

    def bd(xs):
        return [jnp.where(bdm, jnp.concatenate([x.astype(BF16)] * G, axis=0), 0) for x in xs]

    cum = each(lambda x: _split_dot_left(tri, x), ld)
    last = [c[C - 1:C, :] for c in cum]
    b = each(jnp.multiply, kk, a)
    einv = [jnp.exp(-c) for c in cum]
    etail = each(lambda l, c: jnp.exp(l - c), last, cum)
    at = each(lambda x, c, l: -x * jnp.exp(c - l), kk, cum, ld)
    rt = each(lambda x, c: x * jnp.exp(c), r, cum)
    bt = each(jnp.multiply, b, einv)
    kt = each(jnp.multiply, k, einv)
    bh = each(jnp.multiply, b, etail)
    kh = each(jnp.multiply, k, etail)

    vb = bd(v)
    lhs = each(lambda x, y: jnp.concatenate([x, y], axis=0).astype(BF16), at, rt)
    rhs = each(lambda x, y: jnp.concatenate([x, y], axis=0), bd(bt), bd(kt))
    gram = each(_dot_nt, lhs, rhs)
    a_ab = [jnp.where(strict, x[:C, :W], 0.0) for x in gram]
    a_ak = [jnp.where(strict, x[:C, W:], 0.0).astype(BF16) for x in gram]
    a_rb = [jnp.where(incl, x[C:, :W], 0.0).astype(BF16) for x in gram]
    a_rk = [jnp.where(incl, x[C:, W:], 0.0).astype(BF16) for x in gram]

    p = a_ab
    t = [eye + x for x in a_ab]
    pb = bd(p)
    for _ in range(int(math.log2(C)) - 1):
        p = each(_dot, bfl(p), pb)
        pb = bd(p)
        t = each(lambda x, y: x + _dot(x.astype(BF16), y), t, pb)
    tb = bfl(t)

    wa = each(_dot, tb, bd(at))
    x1 = each(_dot, a_ak, vb)
    uv = each(_dot, tb, bd(x1))
    zb = bfl(z)
    u = each(lambda w, s, x: _dot_nt(w.astype(BF16), s) + x, wa, zb, uv)
    y = each(lambda q, s, x, ub, w, xb: _dot_nt(q.astype(BF16), s) + _dot(x, ub) + _dot(w, xb),
             rt, zb, a_rb, bd(u), a_rk, vb)
    gm = each(lambda x, w, s, q: jnp.where(
        bdm, _dot_tn(jnp.concatenate([x, w], axis=0).astype(BF16),
                     jnp.concatenate([s, q], axis=0).astype(BF16)), 0.0), u, v, bh, kh)
    zn = each(lambda s, l, g_: s * jnp.exp(l) + g_, z, last, gm)
    return y, zn


def _split_dot_left(w, x):
    hi = x.astype(BF16)
    lo = (x - hi.astype(F32)).astype(BF16)
    return _dot(w, hi) + _dot(w, lo)


def _rwkv_masks(C, W):
    trow = lax.broadcasted_iota(jnp.int32, (C, C), 0)
    tcol = lax.broadcasted_iota(jnp.int32, (C, C), 1)
    tri = (trow >= tcol).astype(BF16)
    brow = lax.broadcasted_iota(jnp.int32, (W, W), 0)
    bcol = lax.broadcasted_iota(jnp.int32, (W, W), 1)
    bdm = (brow // C) == (bcol // RW_HEAD)
    lrow = lax.broadcasted_iota(jnp.int32, (C, W), 0)
    lcol = lax.broadcasted_iota(jnp.int32, (C, W), 1) % RW_HEAD
    eye = jnp.where(lrow == lcol, 1.0, 0.0)
    return tri, bdm, lrow > lcol, lrow >= lcol, eye


def _rwkv_chunk_kernel(r_ref, ld_ref, k_ref, v_ref, kk_ref, a_ref, g_ref, vec_ref, o_ref, z_ref):
    @pl.when(pl.program_id(2) == 0)
    def _():
        z_ref[...] = jnp.zeros_like(z_ref)

    C = r_ref.shape[0]
    W = RW_GROUP * RW_HEAD
    masks = _rwkv_masks(C, W)
    hsum = _head_ones(W, RW_HEAD)
    inv_n = 1.0 / RW_HEAD
    ng = r_ref.shape[1] // W
    sls = [slice(gi * W, (gi + 1) * W) for gi in range(ng)]
    take = lambda ref: [ref[:, sl].astype(F32) for sl in sls]
    r, k, v = take(r_ref), take(k_ref), take(v_ref)
    ys, zs = _rwkv_group(r, take(ld_ref), k, v, take(kk_ref), take(a_ref),
                         [z_ref[gi] for gi in range(ng)], masks)
    for gi in range(ng):
        z_ref[gi] = zs[gi]
    mean = [_dot(y.astype(BF16), hsum) * inv_n for y in ys]
    yc = [y - mu for y, mu in zip(ys, mean)]
    var = [_dot((x * x).astype(BF16), hsum) * inv_n for x in yc]
    for gi, sl in enumerate(sls):
        vec = vec_ref[:, sl]
        r_k, lnx_w, lnx_b = vec[0:1, :], vec[1:2, :], vec[2:3, :]
        yn = yc[gi] * lax.rsqrt(var[gi] + RW_GN_EPS) * lnx_w + lnx_b
        bonus = _dot((r[gi] * k[gi] * r_k).astype(BF16), hsum) * v[gi]
        o_ref[:, sl] = ((yn + bonus) * g_ref[:, sl].astype(F32)).astype(o_ref.dtype)


def rwkv_scan(r, ld, k, v, kk, a, g, r_k, lnx_w, lnx_b, batch, seq):
    m, d = r.shape
    C = RW_CHUNK
    W = RW_GROUP * RW_HEAD
    assert RW_GROUP * C == W
    wb = min(RW_STEP_GROUPS * W, d)
    nc = seq // C
    zeros = jnp.zeros((d,), F32)
    vec = jnp.stack([r_k.reshape(d), lnx_w, lnx_b] + [zeros] * (SUBLANE - 3))
    spec = pl.BlockSpec((C, wb), lambda b, g, c: (b * nc + c, g))
    return pl.pallas_call(
        _rwkv_chunk_kernel,
        out_shape=jax.ShapeDtypeStruct((m, d), BF16),
        grid=(batch, d // wb, nc),
        in_specs=[spec] * 7 + [pl.BlockSpec((SUBLANE, wb), lambda b, g, c: (0, g))],
        out_specs=spec,
        scratch_shapes=[pltpu.VMEM((wb // W, W, W), F32)],
        compiler_params=_params("parallel", "parallel", "arbitrary"),
        name="rwkv_scan",
    )(r, ld, k, v, kk, a, g, vec)


def _retention_kernel(q_ref, k_ref, v_ref, g_ref, cos_ref, sin_ref, intra_ref, cross_ref,
                      tail_ref, gnw_ref, o_ref, st_ref):
    c = pl.program_id(2)

    @pl.when(c == 0)
    def _():
        st_ref[...] = jnp.zeros_like(st_ref)

    C = q_ref.shape[0]
    half = RET_QK_HEAD // 2
    cos = cos_ref[...]
    sin = sin_ref[...]

    def rope(t):
        t1 = t[:, :half]
        t2 = t[:, half:]
        return jnp.concatenate([t1 * cos - t2 * sin, t1 * sin + t2 * cos], axis=-1)

    hs = range(q_ref.shape[1] // RET_QK_HEAD)
    qk = lambda ref, h: ref[:, h * RET_QK_HEAD:(h + 1) * RET_QK_HEAD].astype(F32)
    vsl = lambda h: slice(h * RET_V_HEAD, (h + 1) * RET_V_HEAD)
    qb = [rope(qk(q_ref, h)).astype(BF16) for h in hs]
    k = [rope(qk(k_ref, h)) * (RET_QK_HEAD ** -0.5) for h in hs]
    vb = [v_ref[:, vsl(h)].astype(BF16) for h in hs]
    cross = [cross_ref[h] for h in hs]
    scores = [(_dot_nt(qb[h], k[h].astype(BF16)) * intra_ref[h]).astype(BF16) for h in hs]
    st = [st_ref[h] for h in hs]
    o = [_dot(scores[h], vb[h]) + _dot(qb[h], st[h].astype(BF16)) * cross[h] for h in hs]
    for h in hs:
        st_ref[h] = st[h] * cross[h][C - 1:C, :] + _dot_tn((k[h] * tail_ref[h]).astype(BF16), vb[h])
    for h in hs:
        mu = jnp.mean(o[h], axis=-1, keepdims=True)
        oc = o[h] - mu
        var = jnp.mean(oc * oc, axis=-1, keepdims=True)
        on = oc * lax.rsqrt(var + GN_EPS) * gnw_ref[:, vsl(h)]
        g = g_ref[:, vsl(h)].astype(F32)
        o_ref[:, vsl(h)] = (g * jax.nn.sigmoid(g) * on).astype(o_ref.dtype)


def retention(proj, gn_w, batch, seq):
    m, n = proj.shape
    heads = n // (2 * RET_QK_HEAD + 2 * RET_V_HEAD)
    C = RET_CHUNK
    nc = seq // C
    half = RET_QK_HEAD // 2
    pos = jnp.arange(seq, dtype=F32)
    inv_freq = 1.0 / (ROPE_BASE ** (jnp.arange(0, RET_QK_HEAD, 2, dtype=F32) / RET_QK_HEAD))
    ang = pos[:, None] * inv_freq[None, :]
    cos, sin = jnp.cos(ang), jnp.sin(ang)
    log_gamma = jnp.log1p(-jnp.exp2(-5.0 - jnp.arange(heads, dtype=F32)))
    cp = jnp.arange(C, dtype=F32)
    rel = cp[:, None] - cp[None, :]
    intra = jnp.where(rel >= 0, jnp.exp(log_gamma[:, None, None] * jnp.maximum(rel, 0.0)), 0.0)
    cross = jnp.exp(log_gamma[:, None] * (cp + 1.0))[:, :, None]
    tail = jnp.exp(log_gamma[:, None] * (C - 1.0 - cp))[:, :, None]
    vdim = heads * RET_V_HEAD
    nh = min(RET_STEP_HEADS, heads)
    nb = heads // nh
    qw, vw = nh * RET_QK_HEAD, nh * RET_V_HEAD
    return pl.pallas_call(
        _retention_kernel,
        out_shape=jax.ShapeDtypeStruct((m, vdim), BF16),
        grid=(batch, nb, nc),
        in_specs=[pl.BlockSpec((C, qw), lambda b, h, c: (b * nc + c, h)),
                  pl.BlockSpec((C, qw), lambda b, h, c: (b * nc + c, nb + h)),
                  pl.BlockSpec((C, vw), lambda b, h, c: (b * nc + c, nb + h)),
                  pl.BlockSpec((C, vw), lambda b, h, c: (b * nc + c, 2 * nb + h)),
                  pl.BlockSpec((C, half), lambda b, h, c: (c, 0)),
                  pl.BlockSpec((C, half), lambda b, h, c: (c, 0)),
                  pl.BlockSpec((nh, C, C), lambda b, h, c: (h, 0, 0)),
                  pl.BlockSpec((nh, C, 1), lambda b, h, c: (h, 0, 0)),
                  pl.BlockSpec((nh, C, 1), lambda b, h, c: (h, 0, 0)),
                  pl.BlockSpec((1, vw), lambda b, h, c: (0, h))],
        out_specs=pl.BlockSpec((C, vw), lambda b, h, c: (b * nc + c, h)),
        scratch_shapes=[pltpu.VMEM((nh, RET_QK_HEAD, RET_V_HEAD), F32)],
        compiler_params=_params("parallel", "parallel", "arbitrary"),
        name="retention",
    )(proj, proj, proj, proj, cos, sin, intra, cross, tail, gn_w.reshape(1, vdim))


def _hgrn_tables(C, wd):
    r = jnp.arange(C)
    gather, bias_q, bias_k, masks = [], [], [], []
    b = C // 2
    while b >= 1:
        ref = (r // (2 * b)) * (2 * b) + b - 1
        gather.append(jax.nn.one_hot(ref, C, dtype=BF16))
        upper = (r & b) != 0
        bias_q.append(jnp.where(upper, 0.0, -jnp.inf))
        bias_k.append(jnp.where(upper, -jnp.inf, 0.0))
        masks.append((r[:, None] // (2 * b)) == (r[None, :] // (2 * b)))
        b //= 2
    masks.append(r[:, None] == r[None, :])
    wide = lambda t: jnp.broadcast_to(jnp.stack(t).astype(F32)[:, :, None], (len(t), C, wd))
    return (jnp.concatenate(gather, axis=0), wide(bias_q), wide(bias_k),
            jnp.stack(masks).astype(F32))


def _hgrn_kernel(q_ref, f_ref, i_ref, g_ref, lb_ref, nw_ref, gat_ref, bq_ref, bk_ref, msk_ref,
                 o_ref, zt_ref):
    @pl.when(pl.program_id(2) == 0)
    def _():
        zt_ref[...] = jnp.zeros_like(zt_ref)

    C, wd = q_ref.shape
    hs = range(wd // HG_HEAD)
    nl = bq_ref.shape[0]
    lb = lb_ref[...]
    qr = q_ref[...]
    q = qr * jax.nn.sigmoid(qr)
    sig = jax.nn.sigmoid(f_ref[...])
    lf = jnp.log(lb + (1.0 - lb) * sig)
    k = (1.0 - lb) * (1.0 - sig)
    vb = i_ref[...].astype(BF16)

    trow = lax.broadcasted_iota(jnp.int32, (C, C), 0)
    tcol = lax.broadcasted_iota(jnp.int32, (C, C), 1)
    tri = (trow >= tcol).astype(BF16)
    cum = _split_dot_left(tri, lf) * LOG2_E
    last = cum[C - 1:C, :]
    cref = _dot(gat_ref[...], cum.astype(BF16))

    ql, kl = [q.astype(BF16)], [k.astype(BF16)]
    for l in range(nl):
        cr = cref[l * C:(l + 1) * C, :]
        ql.append((q * jnp.exp2(cum - cr + bq_ref[l])).astype(BF16))
        kl.append((k * jnp.exp2(cr - cum + bk_ref[l])).astype(BF16))
    qe = (q * jnp.exp2(cum)).astype(BF16)
    kd = (k * jnp.exp2(last - cum)).astype(BF16)
    dec = jnp.exp2(last)

    ls = [slice(h * HG_HEAD, (h + 1) * HG_HEAD) for h in hs]
    zt = [zt_ref[h] for h in hs]
    s = [_dot_nt(ql[0][:, ls[h]], kl[0][:, ls[h]]) * msk_ref[nl] for h in hs]
    for l in range(nl):
        p = [_dot_nt(ql[l + 1][:, ls[h]], kl[l + 1][:, ls[h]]) for h in hs]
        s = [s[h] + (p[h] if l == 0 else p[h] * msk_ref[l]) for h in hs]
    o = [_dot_nt(qe[:, ls[h]], zt[h].astype(BF16)) + _dot(s[h].astype(BF16), vb[:, ls[h]]) for h in hs]
    for h in hs:
        zt_ref[h] = zt[h] * dec[:, ls[h]] + _dot_tn(vb[:, ls[h]], kd[:, ls[h]])
    on = [o[h] * lax.rsqrt(jnp.mean(o[h] * o[h], axis=-1, keepdims=True) + RMS_EPS) for h in hs]
    gate = nw_ref[...] * jax.nn.sigmoid(g_ref[...])
    o_ref[...] = (jnp.concatenate(on, axis=-1) * gate).astype(o_ref.dtype)


def hgrn_mix(proj, lb, norm_w, batch, seq):
    m, n = proj.shape
    d = n // 4
    wb = min(HG_GROUP * HG_HEAD, d)
    nb = d // wb
    C = HG_CHUNK
    nc = seq // C
    tables = _hgrn_tables(C, wb)

    def spec(off):
        return pl.BlockSpec((C, wb), lambda b, h, c: (b * nc + c, off * nb + h))

    vec = pl.BlockSpec((1, wb), lambda b, h, c: (0, h))
    const = lambda t: pl.BlockSpec(t.shape, lambda b, h, c: (0,) * t.ndim)
    return pl.pallas_call(
        _hgrn_kernel,
        out_shape=jax.ShapeDtypeStruct((m, d), BF16),
        grid=(batch, nb, nc),
        in_specs=[spec(0), spec(1), spec(2), spec(3), vec, vec] + [const(t) for t in tables],
        out_specs=spec(0),
        scratch_shapes=[pltpu.VMEM((wb // HG_HEAD, HG_HEAD, HG_HEAD), F32)],
        compiler_params=_params("parallel", "parallel", "arbitrary"),
        name="hgrn",
    )(proj, proj, proj, proj, lb.reshape(1, d), norm_w.reshape(1, d), *tables)


def kernel(x, norm_mix, norm_ffn, norm_final, ffn_w_up, ffn_conv, ffn_w_down, rw_mu, rw_w_rkv, rw_w0, rw_w1, rw_w2, rw_a0, rw_a1, rw_a2, rw_v0, rw_v1, rw_v2, rw_g1, rw_g2, rw_k_k, rw_k_a, rw_r_k, rw_lnx_w, rw_lnx_b, rw_w_o, ret_w_in, ret_gn_w, ret_w_o, hg_w_in, hg_lb_logits, hg_norm_w, hg_w_o):
    batch, seq, d = x.shape
    depth = norm_mix.shape[0]
    m = batch * seq
    lb_all = jnp.cumsum(jax.nn.softmax(hg_lb_logits.astype(F32), axis=0), axis=0)
    lb_all = lb_all - lb_all[0]
    bf = lambda t: t.astype(BF16)

    h = x.reshape(m, d)
    v_first = None
    for layer in range(depth):
        kind = layer % 3
        j = layer // 3
        if kind == 0:
            p = dict(mu=rw_mu[j], w_rkv=bf(rw_w_rkv[j]), w0=rw_w0[j], w1=bf(rw_w1[j]), w2=bf(rw_w2[j]),
                     a0=rw_a0[j], a1=bf(rw_a1[j]), a2=bf(rw_a2[j]), g1=bf(rw_g1[j]), g2=bf(rw_g2[j]),
                     k_k=rw_k_k[j], k_a=rw_k_a[j])
            if j > 0:
                p.update(v0=rw_v0[j - 1], v1=bf(rw_v1[j - 1]), v2=bf(rw_v2[j - 1]))
            r, k, v, ld, kk, a, g = rwkv_proj(h, norm_mix[layer], p, v_first, seq)
            if v_first is None:
                v_first = v
            mixed = rwkv_scan(r, ld, k, v, kk, a, g, rw_r_k[j], rw_lnx_w[j], rw_lnx_b[j], batch, seq)
            h = matmul_res(mixed, bf(rw_w_o[j]), h)
        elif kind == 1:
            proj = norm_matmul(h, norm_mix[layer], bf(ret_w_in[j]), BF16)
            gated = retention(proj, ret_gn_w[j], batch, seq)
            h = matmul_res(gated, bf(ret_w_o[j]), h)
        else:
            proj = norm_matmul(h, norm_mix[layer], bf(hg_w_in[j]))
            gated = hgrn_mix(proj, lb_all[layer], hg_norm_w[j], batch, seq)
            h = matmul_res(gated, bf(hg_w_o[j]), h)
        final_w = norm_final if layer == depth - 1 else None
        h = ffn_block(h, norm_ffn[layer], *_prep_ffn(ffn_w_up[layer], ffn_conv[layer], ffn_w_down[layer]),
                      seq, final_w)
    return h.reshape(batch, seq, d)
```

```python
import functools
import math

import jax
import jax.numpy as jnp
from jax import lax
from jax.experimental import pallas as pl
from jax.experimental.pallas import tpu as pltpu

F32 = jnp.float32
BF16 = jnp.bfloat16

RMS_EPS = 1e-6
GN_EPS = 1e-5
ROPE_BASE = 10000.0

RW_HEAD = 64
RW_GN_EPS = RW_HEAD * 1e-5
RW_CHUNK = 64
RW_GROUP = 4
RW_STEP_GROUPS = 8
RW_ROW_TILE = 512
RW_PRE_ROWS = 128
RW_COL_TILE = 256
EXP_NEG_HALF = math.exp(-0.5)
LOG2_E = 1.0 / math.log(2.0)

RET_QK_HEAD = 256
RET_V_HEAD = 512
RET_CHUNK = 128
RET_STEP_HEADS = 4

HG_HEAD = 128
HG_CHUNK = 64
HG_STEP_CHUNKS = 2
HG_GROUP = 8

FFN_CONV = 3
LANE = 128
SUBLANE = 8
ROW_TILE = 512
MM_ROW_TILE = 1024
MM_TILE_ELEMS = 2 * 1024 * 1024
NORM_ROWS = 256
FFN_TN = 512

VMEM_LIMIT = 56 * 1024 * 1024


def _dot(a, b):
    return jnp.dot(a, b, preferred_element_type=F32)


def _dot_nt(a, b):
    return lax.dot_general(a, b, (((1,), (1,)), ((), ())), preferred_element_type=F32)


def _dot_tn(a, b):
    return lax.dot_general(a, b, (((0,), (0,)), ((), ())), preferred_element_type=F32)


def _params(*sem):
    return pltpu.CompilerParams(dimension_semantics=sem, vmem_limit_bytes=VMEM_LIMIT)


def _rms(x, w):
    ms = jnp.mean(x * x, axis=-1, keepdims=True)
    return x * lax.rsqrt(ms + RMS_EPS) * w


def _head_ones(n, head):
    row = lax.broadcasted_iota(jnp.int32, (n, n), 0)
    col = lax.broadcasted_iota(jnp.int32, (n, n), 1)
    return ((row // head) == (col // head)).astype(BF16)


def _split_dot(x, w):
    hi = x.astype(BF16)
    lo = (x - hi.astype(F32)).astype(BF16)
    return _dot(hi, w) + _dot(lo, w)


def _shift_rows(u, carry, fresh):
    c = jnp.where(fresh, 0.0, carry)
    row = lax.broadcasted_iota(jnp.int32, (SUBLANE, u.shape[1]), 0)
    r1 = pltpu.roll(u, 1, axis=0)
    r2 = pltpu.roll(u, 2, axis=0)
    top1 = jnp.where(row == 0, c[7:8, :], r1[:SUBLANE])
    top2 = jnp.where(row == 0, c[6:7, :], jnp.where(row == 1, c[7:8, :], r2[:SUBLANE]))
    return (jnp.concatenate([top1, r1[SUBLANE:]], axis=0),
            jnp.concatenate([top2, r2[SUBLANE:]], axis=0))


def _mm_res_kernel(x_ref, w_ref, r_ref, o_ref):
    o_ref[...] = (r_ref[...] + _dot(x_ref[...], w_ref[...])).astype(o_ref.dtype)


def _col_tile(n):
    for t in (1024, 512, 256, 128):
        if n % t == 0:
            return t
    return n


def matmul_res(x, w, res):
    m, k = x.shape
    n = w.shape[1]
    tm = min(MM_ROW_TILE, m)
    tn = min(_col_tile(n), MM_TILE_ELEMS // k)
    return pl.pallas_call(
        _mm_res_kernel,
        out_shape=jax.ShapeDtypeStruct((m, n), F32),
        grid=(m // tm, n // tn),
        in_specs=[pl.BlockSpec((tm, k), lambda i, j: (i, 0)),
                  pl.BlockSpec((k, tn), lambda i, j: (0, j)),
                  pl.BlockSpec((tm, tn), lambda i, j: (i, j))],
        out_specs=pl.BlockSpec((tm, tn), lambda i, j: (i, j)),
        compiler_params=_params("parallel", "parallel"),
        name="matmul_res",
    )(x, w, res)


def _norm_mm_kernel(h_ref, nw_ref, w_ref, o_ref, xs_ref):
    @pl.when(pl.program_id(1) == 0)
    def _():
        tm = h_ref.shape[0]
        rb = min(NORM_ROWS, tm)
        nw = nw_ref[...]
        for r0 in range(0, tm, rb):
            xs_ref[r0:r0 + rb, :] = _rms(h_ref[r0:r0 + rb, :], nw).astype(BF16)

    o_ref[...] = _dot(xs_ref[...], w_ref[...]).astype(o_ref.dtype)


def norm_matmul(h, nw, w, out_dtype=F32):
    m, d = h.shape
    n = w.shape[1]
    tm = min(MM_ROW_TILE, m)
    tn = _col_tile(n)
    return pl.pallas_call(
        _norm_mm_kernel,
        out_shape=jax.ShapeDtypeStruct((m, n), out_dtype),
        grid=(m // tm, n // tn),
        in_specs=[pl.BlockSpec((tm, d), lambda i, j: (i, 0)),
                  pl.BlockSpec((1, d), lambda i, j: (0, 0)),
                  pl.BlockSpec((d, tn), lambda i, j: (0, j))],
        out_specs=pl.BlockSpec((tm, tn), lambda i, j: (i, j)),
        scratch_shapes=[pltpu.VMEM((tm, d), BF16)],
        compiler_params=_params("parallel", "arbitrary"),
        name="norm_matmul",
    )(h, nw.reshape(1, d), w)


def _ffn_kernel(h_ref, nw_ref, wg_ref, wv_ref, cg_ref, cv_ref, wd_ref, fw_ref, o_ref,
                xs_ref, carry_g_ref, carry_v_ref, *, tiles_per_seq, final_norm):
    i = pl.program_id(0)
    j = pl.program_id(1)
    tm = xs_ref.shape[0]

    @pl.when(j == 0)
    def _():
        rb = min(NORM_ROWS, tm)
        nw = nw_ref[...]
        for r0 in range(0, tm, rb):
            h = h_ref[r0:r0 + rb, :]
            xs_ref[r0:r0 + rb, :] = _rms(h, nw).astype(BF16)
            o_ref[r0:r0 + rb, :] = h

    fresh = (i % tiles_per_seq) == 0
    xs = xs_ref[...]

    def conv(w_ref, c_ref, carry_ref):
        u = _dot(xs, w_ref[...])
        u1, u2 = _shift_rows(u, carry_ref[j], fresh)
        carry_ref[j] = u[tm - SUBLANE:, :]
        cw = c_ref[...]
        return u2 * cw[0:1, :] + u1 * cw[1:2, :] + u * cw[2:3, :]

    gate = conv(wg_ref, cg_ref, carry_g_ref)
    val = conv(wv_ref, cv_ref, carry_v_ref)
    act = (gate * jax.nn.sigmoid(gate) * val).astype(BF16)
    o_ref[...] += _dot(act, wd_ref[...])

    if final_norm:
        @pl.when(j == pl.num_programs(1) - 1)
        def _():
            o_ref[...] = _rms(o_ref[...], fw_ref[...])


def ffn_block(h, nw, w_gate, w_val, c_gate, c_val, w_down, seq, final_w=None):
    m, d = h.shape
    f = w_down.shape[0]
    tm = min(ROW_TILE, seq)
    tn = FFN_TN
    nt = f // tn
    kern = functools.partial(_ffn_kernel, tiles_per_seq=seq // tm, final_norm=final_w is not None)
    fw = (nw if final_w is None else final_w).reshape(1, d)
    vec = pl.BlockSpec((1, d), lambda i, j: (0, 0))
    col = lambda rows: pl.BlockSpec((rows, tn), lambda i, j: (0, j))
    return pl.pallas_call(
        kern,
        out_shape=jax.ShapeDtypeStruct((m, d), F32),
        grid=(m // tm, nt),
        in_specs=[pl.BlockSpec((tm, d), lambda i, j: (i, 0)), vec, col(d), col(d),
                  col(FFN_CONV), col(FFN_CONV), pl.BlockSpec((tn, d), lambda i, j: (j, 0)), vec],
        out_specs=pl.BlockSpec((tm, d), lambda i, j: (i, 0)),
        scratch_shapes=[pltpu.VMEM((tm, d), BF16),
                        pltpu.VMEM((nt, SUBLANE, tn), F32),
                        pltpu.VMEM((nt, SUBLANE, tn), F32)],
        compiler_params=_params("arbitrary", "arbitrary"),
        name="ffn",
    )(h, nw.reshape(1, d), w_gate, w_val, c_gate, c_val, w_down, fw)


def _prep_ffn(w_up, w_conv, w_down):
    f = w_down.shape[0]
    pad = -f % FFN_TN
    cols = lambda w: jnp.pad(w, ((0, 0), (0, pad)))
    return (cols(w_up[:, :f]).astype(BF16), cols(w_up[:, f:]).astype(BF16),
            cols(w_conv[:, :f]), cols(w_conv[:, f:]),
            jnp.pad(w_down, ((0, pad), (0, 0))).astype(BF16))


def _rwkv_proj_kernel(*refs, vres, tiles_per_seq):
    refs = list(refs)
    h_ref, nw_ref, mu_ref, wr_ref, wk_ref, wv_ref, w1_ref, a1_ref, g1_ref = refs[:9]
    w2_ref, a2_ref, g2_ref, vec_ref = refs[9:13]
    pos = 13
    if vres:
        v1_ref, v2_ref, vf_ref = refs[pos:pos + 3]
        pos += 3
    r_ref, k_ref, v_ref, ld_ref, kk_ref, a_ref, g_ref = refs[pos:pos + 7]
    pos += 7
    xm_ref, hw_ref, ha_ref, hg_ref = refs[pos:pos + 4]
    pos += 4
    if vres:
        hv_ref = refs[pos]
        pos += 1
    carry_ref = refs[pos]

    i = pl.program_id(0)

    @pl.when(pl.program_id(1) == 0)
    def _():
        tm, d = h_ref.shape
        rb = min(RW_PRE_ROWS, tm)
        fresh = (i % tiles_per_seq) == 0
        prev_last = jnp.where(fresh, 0.0, carry_ref[SUBLANE - 1:SUBLANE, :])
        row = lax.broadcasted_iota(jnp.int32, (rb, d), 0)
        nw = nw_ref[...]
        mu = mu_ref[...]
        for b in range(tm // rb):
            rows = slice(b * rb, (b + 1) * rb)
            xn = _rms(h_ref[rows, :], nw)
            xx = jnp.where(row == 0, prev_last, pltpu.roll(xn, 1, axis=0)) - xn
            prev_last = xn[rb - 1:rb, :]
            for q in range(6):
                xm_ref[q, rows, :] = (xn + xx * mu[q:q + 1, :]).astype(BF16)
        carry_ref[...] = xn[rb - SUBLANE:, :]
        hw_ref[...] = jnp.tanh(_dot(xm_ref[3], w1_ref[...])).astype(BF16)
        ha_ref[...] = _dot(xm_ref[4], a1_ref[...]).astype(BF16)
        hg_ref[...] = jax.nn.sigmoid(_dot(xm_ref[5], g1_ref[...])).astype(BF16)
        if vres:
            hv_ref[...] = _dot(xm_ref[2], v1_ref[...]).astype(BF16)

    vec = vec_ref[...]
    w0, a0, k_k, k_a, v0 = (vec[q:q + 1, :] for q in range(5))
    r = _dot(xm_ref[0], wr_ref[...])
    k = _dot(xm_ref[1], wk_ref[...])
    v = _dot(xm_ref[2], wv_ref[...])
    ld = -jax.nn.sigmoid(w0 + _dot(hw_ref[...], w2_ref[...])) * EXP_NEG_HALF
    a = jax.nn.sigmoid(a0 + _dot(ha_ref[...], a2_ref[...]))
    g = _dot(hg_ref[...], g2_ref[...])
    if vres:
        v = v + (vf_ref[...] - v) * jax.nn.sigmoid(v0 + _dot(hv_ref[...], v2_ref[...]))
    kkr = k * k_k
    ss = _split_dot(kkr * kkr, _head_ones(kkr.shape[1], RW_HEAD))
    r_ref[...] = r.astype(r_ref.dtype)
    k_ref[...] = (k * (1.0 + (a - 1.0) * k_a)).astype(k_ref.dtype)
    v_ref[...] = v
    ld_ref[...] = ld
    kk_ref[...] = (kkr / jnp.maximum(jnp.sqrt(ss), 1e-12)).astype(kk_ref.dtype)
    a_ref[...] = a.astype(a_ref.dtype)
    g_ref[...] = g.astype(g_ref.dtype)


def _pad_cols(w, n):
    return jnp.pad(w, ((0, 0), (0, n - w.shape[1])))


def _pad_rows(w, n):
    return jnp.pad(w, ((0, n - w.shape[0]), (0, 0)))


def rwkv_proj(h, nw, p, v_first, seq):
    m, d = h.shape
    tm = min(RW_ROW_TILE, seq)
    tn = min(RW_COL_TILE, d)
    vres = v_first is not None
    lw, la, lg = (-(-p[n].shape[1] // LANE) * LANE for n in ("w1", "a1", "g1"))
    full = lambda shape: pl.BlockSpec(shape, lambda i, j: (0,) * len(shape))
    col = lambda rows: pl.BlockSpec((rows, tn), lambda i, j: (0, j))
    tile = pl.BlockSpec((tm, tn), lambda i, j: (i, j))
    rkv = lambda q: pl.BlockSpec((None, d, tn), lambda i, j: (q, 0, j))
    zeros = jnp.zeros((d,), F32)
    vec = jnp.stack([p["w0"], p["a0"], p["k_k"], p["k_a"], p["v0"] if vres else zeros,
                     zeros, zeros, zeros])
    mu = jnp.pad(p["mu"], ((0, SUBLANE - p["mu"].shape[0]), (0, 0)))
    args = [h, nw.reshape(1, d), mu, p["w_rkv"], p["w_rkv"], p["w_rkv"],
            _pad_cols(p["w1"], lw), _pad_cols(p["a1"], la), _pad_cols(p["g1"], lg),
            _pad_rows(p["w2"], lw), _pad_rows(p["a2"], la), _pad_rows(p["g2"], lg), vec]
    in_specs = [pl.BlockSpec((tm, d), lambda i, j: (i, 0)), full((1, d)), full((SUBLANE, d)),
                rkv(0), rkv(1), rkv(2), full((d, lw)), full((d, la)), full((d, lg)),
                col(lw), col(la), col(lg), col(SUBLANE)]
    scratch = [pltpu.VMEM((6, tm, d), BF16), pltpu.VMEM((tm, lw), BF16),
               pltpu.VMEM((tm, la), BF16), pltpu.VMEM((tm, lg), BF16)]
    if vres:
        lv = -(-p["v1"].shape[1] // LANE) * LANE
        args += [_pad_cols(p["v1"], lv), _pad_rows(p["v2"], lv), v_first]
        in_specs += [full((d, lv)), col(lv), tile]
        scratch.append(pltpu.VMEM((tm, lv), BF16))
    scratch.append(pltpu.VMEM((SUBLANE, d), F32))
    out = jax.ShapeDtypeStruct((m, d), F32)
    half = jax.ShapeDtypeStruct((m, d), BF16)
    kern = functools.partial(_rwkv_proj_kernel, vres=vres, tiles_per_seq=seq // tm)
    return pl.pallas_call(
        kern,
        out_shape=[half, half, out, out, half, half, half],
        grid=(m // tm, d // tn),
        in_specs=in_specs,
        out_specs=[tile] * 7,
        scratch_shapes=scratch,
        compiler_params=_params("arbitrary", "arbitrary"),
        name="rwkv_proj",
    )(*args)


def _rwkv_group(r, ld, k, v, kk, a, z, masks):
    tri, bdm, strict, incl, eye = masks
    C, W = r[0].shape
    G = W // RW_HEAD
    bfl = lambda xs: [x.astype(BF16) for x in xs]

    def each(f, *ls):
        return [f(*xs) for xs in zip(*ls)]

    def bd(xs):
        return [jnp.where(bdm, jnp.concatenate([x.astype(BF16)] * G, axis=0), 0) for x in xs]

    cum = each(lambda x: _split_dot_left(tri, x), ld)
    last = [c[C - 1:C, :] for c in cum]
    b = each(jnp.multiply, kk, a)
    einv = [jnp.exp(-c) for c in cum]
    etail = each(lambda l, c: jnp.exp(l - c), last, cum)
    at = each(lambda x, c, l: -x * jnp.exp(c - l), kk, cum, ld)
    rt = each(lambda x, c: x * jnp.exp(c), r, cum)
    bt = each(jnp.multiply, b, einv)
    kt = each(jnp.multiply, k, einv)
    bh = each(jnp.multiply, b, etail)
    kh = each(jnp.multiply, k, etail)

    vb = bd(v)
    lhs = each(lambda x, y: jnp.concatenate([x, y], axis=0).astype(BF16), at, rt)
    rhs = each(lambda x, y: jnp.concatenate([x, y], axis=0), bd(bt), bd(kt))
    gram = each(_dot_nt, lhs, rhs)
    a_ab = [jnp.where(strict, x[:C, :W], 0.0) for x in gram]
    a_ak = [jnp.where(strict, x[:C, W:], 0.0).astype(BF16) for x in gram]
    a_rb = [jnp.where(incl, x[C:, :W], 0.0).astype(BF16) for x in gram]
    a_rk = [jnp.where(incl, x[C:, W:], 0.0).astype(BF16) for x in gram]

    rows2 = lambda x, y: jnp.concatenate([x, y], axis=0).astype(BF16)

    t = [eye + x for x in a_ab]
    p = each(_dot, bfl(a_ab), bd(a_ab))
    for _ in range(int(math.log2(C)) - 2):
        st = each(_dot, each(rows2, t, p), bd(p))
        t = each(lambda x, s: x + s[:C], t, st)
        p = [s[C:] for s in st]
    tb = each(lambda x, y: (x + _dot(x.astype(BF16), y)).astype(BF16), t, bd(p))

    wa = each(_dot, tb, bd(at))
    st = each(_dot, each(rows2, a_ak, a_rk), vb)
    uv = each(_dot, tb, bd([s[:C] for s in st]))
    zb = bfl(z)
    sz = each(_dot_nt, each(rows2, wa, rt), zb)
    u = each(lambda s, x: s[:C] + x, sz, uv)
    y = each(lambda s, x, ub, w: s[C:] + _dot(x, ub) + w[C:], sz, a_rb, bd(u), st)
    gm = each(lambda x, w, s, q: jnp.where(
        bdm, _dot_tn(jnp.concatenate([x, w], axis=0).astype(BF16),
                     jnp.concatenate([s, q], axis=0).astype(BF16)), 0.0), u, v, bh, kh)
    zn = each(lambda s, l, g_: s * jnp.exp(l) + g_, z, last, gm)
    return y, zn


def _split_dot_left(w, x):
    hi = x.astype(BF16)
    lo = (x - hi.astype(F32)).astype(BF16)
    return _dot(w, hi) + _dot(w, lo)


def _rwkv_masks(C, W):
    trow = lax.broadcasted_iota(jnp.int32, (C, C), 0)
    tcol = lax.broadcasted_iota(jnp.int32, (C, C), 1)
    tri = (trow >= tcol).astype(BF16)
    brow = lax.broadcasted_iota(jnp.int32, (W, W), 0)
    bcol = lax.broadcasted_iota(jnp.int32, (W, W), 1)
    bdm = (brow // C) == (bcol // RW_HEAD)
    lrow = lax.broadcasted_iota(jnp.int32, (C, W), 0)
    lcol = lax.broadcasted_iota(jnp.int32, (C, W), 1) % RW_HEAD
    eye = jnp.where(lrow == lcol, 1.0, 0.0)
    return tri, bdm, lrow > lcol, lrow >= lcol, eye


def _rwkv_chunk_kernel(r_ref, ld_ref, k_ref, v_ref, kk_ref, a_ref, g_ref, vec_ref, o_ref, z_ref):
    @pl.when(pl.program_id(2) == 0)
    def _():
        z_ref[...] = jnp.zeros_like(z_ref)

    C = r_ref.shape[0]
    W = RW_GROUP * RW_HEAD
    masks = _rwkv_masks(C, W)
    hsum = _head_ones(W, RW_HEAD)
    inv_n = 1.0 / RW_HEAD
    ng = r_ref.shape[1] // W
    sls = [slice(gi * W, (gi + 1) * W) for gi in range(ng)]
    take = lambda ref: [ref[:, sl].astype(F32) for sl in sls]
    r, k, v = take(r_ref), take(k_ref), take(v_ref)
    ys, zs = _rwkv_group(r, take(ld_ref), k, v, take(kk_ref), take(a_ref),
                         [z_ref[gi] for gi in range(ng)], masks)
    for gi in range(ng):
        z_ref[gi] = zs[gi]
    vecs = [vec_ref[:, sl] for sl in sls]
    rk = [r[gi] * k[gi] * vecs[gi][0:1, :] for gi in range(ng)]
    sums = [_dot(jnp.concatenate([ys[gi], rk[gi]], axis=0).astype(BF16), hsum) for gi in range(ng)]
    yc = [ys[gi] - sums[gi][:C] * inv_n for gi in range(ng)]
    var = [_dot((x * x).astype(BF16), hsum) * inv_n for x in yc]
    for gi, sl in enumerate(sls):
        lnx_w, lnx_b = vecs[gi][1:2, :], vecs[gi][2:3, :]
        yn = yc[gi] * lax.rsqrt(var[gi] + RW_GN_EPS) * lnx_w + lnx_b
        bonus = sums[gi][C:] * v[gi]
        o_ref[:, sl] = ((yn + bonus) * g_ref[:, sl].astype(F32)).astype(o_ref.dtype)


def rwkv_scan(r, ld, k, v, kk, a, g, r_k, lnx_w, lnx_b, batch, seq):
    m, d = r.shape
    C = RW_CHUNK
    W = RW_GROUP * RW_HEAD
    assert RW_GROUP * C == W
    wb = min(RW_STEP_GROUPS * W, d)
    nc = seq // C
    zeros = jnp.zeros((d,), F32)
    vec = jnp.stack([r_k.reshape(d), lnx_w, lnx_b] + [zeros] * (SUBLANE - 3))
    spec = pl.BlockSpec((C, wb), lambda b, g, c: (b * nc + c, g))
    return pl.pallas_call(
        _rwkv_chunk_kernel,
        out_shape=jax.ShapeDtypeStruct((m, d), BF16),
        grid=(batch, d // wb, nc),
        in_specs=[spec] * 7 + [pl.BlockSpec((SUBLANE, wb), lambda b, g, c: (0, g))],
        out_specs=spec,
        scratch_shapes=[pltpu.VMEM((wb // W, W, W), F32)],
        compiler_params=_params("parallel", "parallel", "arbitrary"),
        name="rwkv_scan",
    )(r, ld, k, v, kk, a, g, vec)


def _retention_kernel(q_ref, k_ref, v_ref, g_ref, cos_ref, sin_ref, intra_ref, cross_ref,
                      tail_ref, gnw_ref, o_ref, st_ref):
    c = pl.program_id(2)

    @pl.when(c == 0)
    def _():
        st_ref[...] = jnp.zeros_like(st_ref)

    C = q_ref.shape[0]
    half = RET_QK_HEAD // 2
    cos = cos_ref[...]
    sin = sin_ref[...]

    def rope(t):
        t1 = t[:, :half]
        t2 = t[:, half:]
        return jnp.concatenate([t1 * cos - t2 * sin, t1 * sin + t2 * cos], axis=-1)

    hs = range(q_ref.shape[1] // RET_QK_HEAD)
    qk = lambda ref, h: ref[:, h * RET_QK_HEAD:(h + 1) * RET_QK_HEAD].astype(F32)
    vsl = lambda h: slice(h * RET_V_HEAD, (h + 1) * RET_V_HEAD)
    qb = [rope(qk(q_ref, h)).astype(BF16) for h in hs]
    k = [rope(qk(k_ref, h)) * (RET_QK_HEAD ** -0.5) for h in hs]
    vb = [v_ref[:, vsl(h)].astype(BF16) for h in hs]
    cross = [cross_ref[h] for h in hs]
    scores = [(_dot_nt(qb[h], k[h].astype(BF16)) * intra_ref[h]).astype(BF16) for h in hs]
    st = [st_ref[h] for h in hs]
    o = [_dot(scores[h], vb[h]) + _dot(qb[h], st[h].astype(BF16)) * cross[h] for h in hs]
    for h in hs:
        st_ref[h] = st[h] * cross[h][C - 1:C, :] + _dot_tn((k[h] * tail_ref[h]).astype(BF16), vb[h])
    for h in hs:
        mu = jnp.mean(o[h], axis=-1, keepdims=True)
        oc = o[h] - mu
        var = jnp.mean(oc * oc, axis=-1, keepdims=True)
        on = oc * lax.rsqrt(var + GN_EPS) * gnw_ref[:, vsl(h)]
        g = g_ref[:, vsl(h)].astype(F32)
        o_ref[:, vsl(h)] = (g * jax.nn.sigmoid(g) * on).astype(o_ref.dtype)


def retention(proj, gn_w, batch, seq):
    m, n = proj.shape
    heads = n // (2 * RET_QK_HEAD + 2 * RET_V_HEAD)
    C = RET_CHUNK
    nc = seq // C
    half = RET_QK_HEAD // 2
    pos = jnp.arange(seq, dtype=F32)
    inv_freq = 1.0 / (ROPE_BASE ** (jnp.arange(0, RET_QK_HEAD, 2, dtype=F32) / RET_QK_HEAD))
    ang = pos[:, None] * inv_freq[None, :]
    cos, sin = jnp.cos(ang), jnp.sin(ang)
    log_gamma = jnp.log1p(-jnp.exp2(-5.0 - jnp.arange(heads, dtype=F32)))
    cp = jnp.arange(C, dtype=F32)
    rel = cp[:, None] - cp[None, :]
    intra = jnp.where(rel >= 0, jnp.exp(log_gamma[:, None, None] * jnp.maximum(rel, 0.0)), 0.0)
    cross = jnp.exp(log_gamma[:, None] * (cp + 1.0))[:, :, None]
    tail = jnp.exp(log_gamma[:, None] * (C - 1.0 - cp))[:, :, None]
    vdim = heads * RET_V_HEAD
    nh = min(RET_STEP_HEADS, heads)
    nb = heads // nh
    qw, vw = nh * RET_QK_HEAD, nh * RET_V_HEAD
    return pl.pallas_call(
        _retention_kernel,
        out_shape=jax.ShapeDtypeStruct((m, vdim), BF16),
        grid=(batch, nb, nc),
        in_specs=[pl.BlockSpec((C, qw), lambda b, h, c: (b * nc + c, h)),
                  pl.BlockSpec((C, qw), lambda b, h, c: (b * nc + c, nb + h)),
                  pl.BlockSpec((C, vw), lambda b, h, c: (b * nc + c, nb + h)),
                  pl.BlockSpec((C, vw), lambda b, h, c: (b * nc + c, 2 * nb + h)),
                  pl.BlockSpec((C, half), lambda b, h, c: (c, 0)),
                  pl.BlockSpec((C, half), lambda b, h, c: (c, 0)),
                  pl.BlockSpec((nh, C, C), lambda b, h, c: (h, 0, 0)),
                  pl.BlockSpec((nh, C, 1), lambda b, h, c: (h, 0, 0)),
                  pl.BlockSpec((nh, C, 1), lambda b, h, c: (h, 0, 0)),
                  pl.BlockSpec((1, vw), lambda b, h, c: (0, h))],
        out_specs=pl.BlockSpec((C, vw), lambda b, h, c: (b * nc + c, h)),
        scratch_shapes=[pltpu.VMEM((nh, RET_QK_HEAD, RET_V_HEAD), F32)],
        compiler_params=_params("parallel", "parallel", "arbitrary"),
        name="retention",
    )(proj, proj, proj, proj, cos, sin, intra, cross, tail, gn_w.reshape(1, vdim))


def _hgrn_tables(C, wd):
    r = jnp.arange(C)
    gather, bias_q, bias_k, masks = [], [], [], []
    b = C // 2
    while b >= 1:
        ref = (r // (2 * b)) * (2 * b) + b - 1
        gather.append(jax.nn.one_hot(ref, C, dtype=BF16))
        upper = (r & b) != 0
        bias_q.append(jnp.where(upper, 0.0, -jnp.inf))
        bias_k.append(jnp.where(upper, -jnp.inf, 0.0))
        masks.append((r[:, None] // (2 * b)) == (r[None, :] // (2 * b)))
        b //= 2
    masks.append(r[:, None] == r[None, :])
    wide = lambda t: jnp.broadcast_to(jnp.stack(t).astype(F32)[:, :, None], (len(t), C, wd))
    return (jnp.concatenate(gather, axis=0), wide(bias_q), wide(bias_k),
            jnp.stack(masks).astype(F32))


def _hgrn_kernel(q_ref, f_ref, i_ref, g_ref, lb_ref, nw_ref, gat_ref, bq_ref, bk_ref, msk_ref,
                 o_ref, zt_ref):
    @pl.when(pl.program_id(2) == 0)
    def _():
        zt_ref[...] = jnp.zeros_like(zt_ref)

    rows, wd = q_ref.shape
    C = gat_ref.shape[1]
    nl = bq_ref.shape[0]
    heads = [slice(h * HG_HEAD, (h + 1) * HG_HEAD) for h in range(wd // HG_HEAD)]
    chunks = [slice(r0, r0 + C) for r0 in range(0, rows, C)]
    lb = lb_ref[...]
    qr = q_ref[...]
    q = qr * jax.nn.sigmoid(qr)
    sig = jax.nn.sigmoid(f_ref[...])
    lf = jnp.log(lb + (1.0 - lb) * sig)
    k = (1.0 - lb) * (1.0 - sig)
    vb = i_ref[...].astype(BF16)

    trow = lax.broadcasted_iota(jnp.int32, (rows, rows), 0)
    tcol = lax.broadcasted_iota(jnp.int32, (rows, rows), 1)
    tri = ((trow >= tcol) & ((trow // C) == (tcol // C))).astype(BF16)
    cum = _split_dot_left(tri, lf) * LOG2_E
    cb = cum.astype(BF16)

    cref = [_dot(gat_ref[...], cb[c, :]) for c in chunks]
    s = [[_dot_nt(q[c, h].astype(BF16), k[c, h].astype(BF16)) * msk_ref[nl] for h in heads]
         for c in chunks]
    for l in range(nl):
        lv = slice(l * C, (l + 1) * C)
        ql = [(q[c, :] * jnp.exp2(cum[c, :] - cr[lv, :] + bq_ref[l])).astype(BF16)
              for c, cr in zip(chunks, cref)]
        kl = [(k[c, :] * jnp.exp2(cr[lv, :] - cum[c, :] + bk_ref[l])).astype(BF16)
              for c, cr in zip(chunks, cref)]
        p = [[_dot_nt(ql[i][:, h], kl[i][:, h]) for h in heads] for i in range(len(chunks))]
        s = [[s[i][j] + (p[i][j] if l == 0 else p[i][j] * msk_ref[l]) for j in range(len(heads))]
             for i in range(len(chunks))]
    intra = [[_dot(s[i][j].astype(BF16), vb[c, h]) for j, h in enumerate(heads)]
             for i, c in enumerate(chunks)]

    gate = nw_ref[...] * jax.nn.sigmoid(g_ref[...])
    zt = [zt_ref[j] for j in range(len(heads))]
    for i, c in enumerate(chunks):
        cc = cum[c, :]
        last = cc[C - 1:C, :]
        qe = (q[c, :] * jnp.exp2(cc)).astype(BF16)
        kd = (k[c, :] * jnp.exp2(last - cc)).astype(BF16)
        dec = jnp.exp2(last)
        o = [_dot_nt(qe[:, h], zt[j].astype(BF16)) + intra[i][j] for j, h in enumerate(heads)]
        zt = [zt[j] * dec[:, h] + _dot_tn(vb[c, h], kd[:, h]) for j, h in enumerate(heads)]
        on = [x * lax.rsqrt(jnp.mean(x * x, axis=-1, keepdims=True) + RMS_EPS) for x in o]
        o_ref[c, :] = (jnp.concatenate(on, axis=-1) * gate[c, :]).astype(o_ref.dtype)
    for j in range(len(heads)):
        zt_ref[j] = zt[j]


def hgrn_mix(proj, lb, norm_w, batch, seq):
    m, n = proj.shape
    d = n // 4
    wb = min(HG_GROUP * HG_HEAD, d)
    nb = d // wb
    rows = min(HG_STEP_CHUNKS * HG_CHUNK, seq)
    nc = seq // rows
    tables = _hgrn_tables(HG_CHUNK, wb)

    def spec(off):
        return pl.BlockSpec((rows, wb), lambda b, h, c: (b * nc + c, off * nb + h))

    vec = pl.BlockSpec((1, wb), lambda b, h, c: (0, h))
    const = lambda t: pl.BlockSpec(t.shape, lambda b, h, c: (0,) * t.ndim)
    return pl.pallas_call(
        _hgrn_kernel,
        out_shape=jax.ShapeDtypeStruct((m, d), BF16),
        grid=(batch, nb, nc),
        in_specs=[spec(0), spec(1), spec(2), spec(3), vec, vec] + [const(t) for t in tables],
        out_specs=spec(0),
        scratch_shapes=[pltpu.VMEM((wb // HG_HEAD, HG_HEAD, HG_HEAD), F32)],
        compiler_params=_params("parallel", "parallel", "arbitrary"),
        name="hgrn",
    )(proj, proj, proj, proj, lb.reshape(1, d), norm_w.reshape(1, d), *tables)


def kernel(x, norm_mix, norm_ffn, norm_final, ffn_w_up, ffn_conv, ffn_w_down, rw_mu, rw_w_rkv, rw_w0, rw_w1, rw_w2, rw_a0, rw_a1, rw_a2, rw_v0, rw_v1, rw_v2, rw_g1, rw_g2, rw_k_k, rw_k_a, rw_r_k, rw_lnx_w, rw_lnx_b, rw_w_o, ret_w_in, ret_gn_w, ret_w_o, hg_w_in, hg_lb_logits, hg_norm_w, hg_w_o):
    batch, seq, d = x.shape
    depth = norm_mix.shape[0]
    m = batch * seq
    lb_all = jnp.cumsum(jax.nn.softmax(hg_lb_logits.astype(F32), axis=0), axis=0)
    lb_all = lb_all - lb_all[0]
    bf = lambda t: t.astype(BF16)

    h = x.reshape(m, d)
    v_first = None
    for layer in range(depth):
        kind = layer % 3
        j = layer // 3
        if kind == 0:
            p = dict(mu=rw_mu[j], w_rkv=bf(rw_w_rkv[j]), w0=rw_w0[j], w1=bf(rw_w1[j]), w2=bf(rw_w2[j]),
                     a0=rw_a0[j], a1=bf(rw_a1[j]), a2=bf(rw_a2[j]), g1=bf(rw_g1[j]), g2=bf(rw_g2[j]),
                     k_k=rw_k_k[j], k_a=rw_k_a[j])
            if j > 0:
                p.update(v0=rw_v0[j - 1], v1=bf(rw_v1[j - 1]), v2=bf(rw_v2[j - 1]))
            r, k, v, ld, kk, a, g = rwkv_proj(h, norm_mix[layer], p, v_first, seq)
            if v_first is None:
                v_first = v
            mixed = rwkv_scan(r, ld, k, v, kk, a, g, rw_r_k[j], rw_lnx_w[j], rw_lnx_b[j], batch, seq)
            h = matmul_res(mixed, bf(rw_w_o[j]), h)
        elif kind == 1:
            proj = norm_matmul(h, norm_mix[layer], bf(ret_w_in[j]), BF16)
            gated = retention(proj, ret_gn_w[j], batch, seq)
            h = matmul_res(gated, bf(ret_w_o[j]), h)
        else:
            proj = norm_matmul(h, norm_mix[layer], bf(hg_w_in[j]))
            gated = hgrn_mix(proj, lb_all[layer], hg_norm_w[j], batch, seq)
            h = matmul_res(gated, bf(hg_w_o[j]), h)
        final_w = norm_final if layer == depth - 1 else None
        h = ffn_block(h, norm_ffn[layer], *_prep_ffn(ffn_w_up[layer], ffn_conv[layer], ffn_w_down[layer]),
                      seq, final_w)
    return h.reshape(batch, seq, d)
```

```python
import functools
import math

import jax
import jax.numpy as jnp
from jax import lax
from jax.experimental import pallas as pl
from jax.experimental.pallas import tpu as pltpu

F32 = jnp.float32
BF16 = jnp.bfloat16

RMS_EPS = 1e-6
GN_EPS = 1e-5
ROPE_BASE = 10000.0

RW_HEAD = 64
RW_GN_EPS = RW_HEAD * 1e-5
RW_CHUNK = 64
RW_GROUP = 4
RW_STEP_GROUPS = 8
RW_ROW_TILE = 512
RW_PRE_ROWS = 128
RW_COL_TILE = 256
EXP_NEG_HALF = math.exp(-0.5)
LOG2_E = 1.0 / math.log(2.0)

RET_QK_HEAD = 256
RET_V_HEAD = 512
RET_CHUNK = 128
RET_STEP_HEADS = 4

HG_HEAD = 128
HG_CHUNK = 64
HG_STEP_CHUNKS = 2
HG_GROUP = 8

FFN_CONV = 3
LANE = 128
SUBLANE = 8
ROW_TILE = 512
MM_ROW_TILE = 1024
MM_TILE_ELEMS = 2 * 1024 * 1024
NORM_ROWS = 256
FFN_TN = 512

VMEM_LIMIT = 56 * 1024 * 1024


def _dot(a, b):
    return jnp.dot(a, b, preferred_element_type=F32)


def _dot_nt(a, b):
    return lax.dot_general(a, b, (((1,), (1,)), ((), ())), preferred_element_type=F32)


def _dot_tn(a, b):
    return lax.dot_general(a, b, (((0,), (0,)), ((), ())), preferred_element_type=F32)


def _params(*sem):
    return pltpu.CompilerParams(dimension_semantics=sem, vmem_limit_bytes=VMEM_LIMIT)


def _rms(x, w):
    ms = jnp.mean(x * x, axis=-1, keepdims=True)
    return x * lax.rsqrt(ms + RMS_EPS) * w


def _head_ones(n, head):
    row = lax.broadcasted_iota(jnp.int32, (n, n), 0)
    col = lax.broadcasted_iota(jnp.int32, (n, n), 1)
    return ((row // head) == (col // head)).astype(BF16)


def _split_dot(x, w):
    hi = x.astype(BF16)
    lo = (x - hi.astype(F32)).astype(BF16)
    return _dot(hi, w) + _dot(lo, w)


def _shift_rows(u, carry, fresh):
    c = jnp.where(fresh, 0.0, carry)
    row = lax.broadcasted_iota(jnp.int32, (SUBLANE, u.shape[1]), 0)
    r1 = pltpu.roll(u, 1, axis=0)
    r2 = pltpu.roll(u, 2, axis=0)
    top1 = jnp.where(row == 0, c[7:8, :], r1[:SUBLANE])
    top2 = jnp.where(row == 0, c[6:7, :], jnp.where(row == 1, c[7:8, :], r2[:SUBLANE]))
    return (jnp.concatenate([top1, r1[SUBLANE:]], axis=0),
            jnp.concatenate([top2, r2[SUBLANE:]], axis=0))


def _mm_res_kernel(x_ref, w_ref, r_ref, o_ref):
    o_ref[...] = (r_ref[...] + _dot(x_ref[...], w_ref[...])).astype(o_ref.dtype)


def _col_tile(n):
    for t in (1024, 512, 256, 128):
        if n % t == 0:
            return t
    return n


def matmul_res(x, w, res):
    m, k = x.shape
    n = w.shape[1]
    tm = min(MM_ROW_TILE, m)
    tn = min(_col_tile(n), MM_TILE_ELEMS // k)
    return pl.pallas_call(
        _mm_res_kernel,
        out_shape=jax.ShapeDtypeStruct((m, n), F32),
        grid=(m // tm, n // tn),
        in_specs=[pl.BlockSpec((tm, k), lambda i, j: (i, 0)),
                  pl.BlockSpec((k, tn), lambda i, j: (0, j)),
                  pl.BlockSpec((tm, tn), lambda i, j: (i, j))],
        out_specs=pl.BlockSpec((tm, tn), lambda i, j: (i, j)),
        compiler_params=_params("parallel", "parallel"),
        name="matmul_res",
    )(x, w, res)


def _norm_mm_kernel(h_ref, nw_ref, w_ref, o_ref, xs_ref):
    @pl.when(pl.program_id(1) == 0)
    def _():
        tm = h_ref.shape[0]
        rb = min(NORM_ROWS, tm)
        nw = nw_ref[...]
        for r0 in range(0, tm, rb):
            xs_ref[r0:r0 + rb, :] = _rms(h_ref[r0:r0 + rb, :], nw).astype(BF16)

    o_ref[...] = _dot(xs_ref[...], w_ref[...]).astype(o_ref.dtype)


def norm_matmul(h, nw, w, out_dtype=F32):
    m, d = h.shape
    n = w.shape[1]
    tm = min(MM_ROW_TILE, m)
    tn = _col_tile(n)
    return pl.pallas_call(
        _norm_mm_kernel,
        out_shape=jax.ShapeDtypeStruct((m, n), out_dtype),
        grid=(m // tm, n // tn),
        in_specs=[pl.BlockSpec((tm, d), lambda i, j: (i, 0)),
                  pl.BlockSpec((1, d), lambda i, j: (0, 0)),
                  pl.BlockSpec((d, tn), lambda i, j: (0, j))],
        out_specs=pl.BlockSpec((tm, tn), lambda i, j: (i, j)),
        scratch_shapes=[pltpu.VMEM((tm, d), BF16)],
        compiler_params=_params("parallel", "arbitrary"),
        name="norm_matmul",
    )(h, nw.reshape(1, d), w)


def _ffn_kernel(hn_ref, hr_ref, nw_ref, wg_ref, wv_ref, cg_ref, cv_ref, wd_ref, fw_ref, o_ref,
                xs_ref, a0_ref, a1_ref, carry_g_ref, carry_v_ref, *, nt, tiles, tiles_per_seq, final_norm):
    t = pl.program_id(0)
    up = jnp.minimum(t, tiles - 1)
    ju = up % nt
    dn = jnp.maximum(t - 1, 0)
    jd = dn % nt
    tm = xs_ref.shape[0]

    @pl.when(t == 0)
    def _():
        a0_ref[...] = jnp.zeros_like(a0_ref)
        a1_ref[...] = jnp.zeros_like(a1_ref)

    @pl.when((t % nt == 0) & (t < tiles))
    def _():
        rb = min(NORM_ROWS, tm)
        nw = nw_ref[...]
        for r0 in range(0, tm, rb):
            xs_ref[r0:r0 + rb, :] = _rms(hn_ref[r0:r0 + rb, :], nw).astype(BF16)

    @pl.when(jd == 0)
    def _():
        o_ref[...] = hr_ref[...]

    fresh = ((up // nt) % tiles_per_seq) == 0

    def conv(xs, w_ref, c_ref, carry_ref):
        u = _dot(xs, w_ref[...])
        u1, u2 = _shift_rows(u, carry_ref[ju], fresh)
        carry_ref[ju] = u[tm - SUBLANE:, :]
        cw = c_ref[...]
        return u2 * cw[0:1, :] + u1 * cw[1:2, :] + u * cw[2:3, :]

    def stages(a_up, a_dn):
        xs = xs_ref[...]
        gate = conv(xs, wg_ref, cg_ref, carry_g_ref)
        val = conv(xs, wv_ref, cv_ref, carry_v_ref)
        a_up[...] = (gate * jax.nn.sigmoid(gate) * val).astype(BF16)
        o_ref[...] += _dot(a_dn[...], wd_ref[...])

    @pl.when(t % 2 == 0)
    def _():
        stages(a0_ref, a1_ref)

    @pl.when(t % 2 == 1)
    def _():
        stages(a1_ref, a0_ref)

    if final_norm:
        @pl.when((jd == nt - 1) & (t > 0))
        def _():
            o_ref[...] = _rms(o_ref[...], fw_ref[...])


def ffn_block(h, nw, w_gate, w_val, c_gate, c_val, w_down, seq, final_w=None):
    m, d = h.shape
    f = w_down.shape[0]
    tm = min(ROW_TILE, seq)
    tn = FFN_TN
    nt = f // tn
    tiles = (m // tm) * nt
    kern = functools.partial(_ffn_kernel, nt=nt, tiles=tiles, tiles_per_seq=seq // tm,
                             final_norm=final_w is not None)
    fw = (nw if final_w is None else final_w).reshape(1, d)
    up_t = lambda t: jnp.minimum(t, tiles - 1)
    dn_t = lambda t: jnp.maximum(t - 1, 0)
    vec = pl.BlockSpec((1, d), lambda t: (0, 0))
    col = lambda rows: pl.BlockSpec((rows, tn), lambda t: (0, up_t(t) % nt))
    return pl.pallas_call(
        kern,
        out_shape=jax.ShapeDtypeStruct((m, d), F32),
        grid=(tiles + 1,),
        in_specs=[pl.BlockSpec((tm, d), lambda t: (up_t(t) // nt, 0)),
                  pl.BlockSpec((tm, d), lambda t: (dn_t(t) // nt, 0)),
                  vec, col(d), col(d), col(FFN_CONV), col(FFN_CONV),
                  pl.BlockSpec((tn, d), lambda t: (dn_t(t) % nt, 0)), vec],
        out_specs=pl.BlockSpec((tm, d), lambda t: (dn_t(t) // nt, 0)),
        scratch_shapes=[pltpu.VMEM((tm, d), BF16),
                        pltpu.VMEM((tm, tn), BF16), pltpu.VMEM((tm, tn), BF16),
                        pltpu.VMEM((nt, SUBLANE, tn), F32),
                        pltpu.VMEM((nt, SUBLANE, tn), F32)],
        compiler_params=_params("arbitrary"),
        name="ffn",
    )(h, h, nw.reshape(1, d), w_gate, w_val, c_gate, c_val, w_down, fw)


def _prep_ffn(w_up, w_conv, w_down):
    f = w_down.shape[0]
    pad = -f % FFN_TN
    cols = lambda w, dt: jnp.concatenate([w.astype(dt), jnp.zeros((w.shape[0], pad), dt)], axis=1)
    down = jnp.concatenate([w_down.astype(BF16), jnp.zeros((pad, w_down.shape[1]), BF16)], axis=0)
    return (cols(w_up[:, :f], BF16), cols(w_up[:, f:], BF16),
            cols(w_conv[:, :f], F32), cols(w_conv[:, f:], F32), down)


def _rwkv_proj_kernel(*refs, vres, tiles_per_seq):
    refs = list(refs)
    h_ref, nw_ref, mu_ref, wr_ref, wk_ref, wv_ref, w1_ref, a1_ref, g1_ref = refs[:9]
    w2_ref, a2_ref, g2_ref, vec_ref = refs[9:13]
    pos = 13
    if vres:
        v1_ref, v2_ref, vf_ref = refs[pos:pos + 3]
        pos += 3
    r_ref, k_ref, v_ref, ld_ref, kk_ref, a_ref, g_ref = refs[pos:pos + 7]
    pos += 7
    xm_ref, hw_ref, ha_ref, hg_ref = refs[pos:pos + 4]
    pos += 4
    if vres:
        hv_ref = refs[pos]
        pos += 1
    carry_ref = refs[pos]

    i = pl.program_id(0)

    @pl.when(pl.program_id(1) == 0)
    def _():
        tm, d = h_ref.shape
        rb = min(RW_PRE_ROWS, tm)
        fresh = (i % tiles_per_seq) == 0
        prev_last = jnp.where(fresh, 0.0, carry_ref[SUBLANE - 1:SUBLANE, :])
        row = lax.broadcasted_iota(jnp.int32, (rb, d), 0)
        nw = nw_ref[...]
        mu = mu_ref[...]
        for b in range(tm // rb):
            rows = slice(b * rb, (b + 1) * rb)
            xn = _rms(h_ref[rows, :], nw)
            xx = jnp.where(row == 0, prev_last, pltpu.roll(xn, 1, axis=0)) - xn
            prev_last = xn[rb - 1:rb, :]
            for q in range(6):
                xm_ref[q, rows, :] = (xn + xx * mu[q:q + 1, :]).astype(BF16)
        carry_ref[...] = xn[rb - SUBLANE:, :]
        hw_ref[...] = jnp.tanh(_dot(xm_ref[3], w1_ref[...])).astype(BF16)
        ha_ref[...] = _dot(xm_ref[4], a1_ref[...]).astype(BF16)
        hg_ref[...] = jax.nn.sigmoid(_dot(xm_ref[5], g1_ref[...])).astype(BF16)
        if vres:
            hv_ref[...] = _dot(xm_ref[2], v1_ref[...]).astype(BF16)

    vec = vec_ref[...]
    w0, a0, k_k, k_a, v0 = (vec[q:q + 1, :] for q in range(5))
    r = _dot(xm_ref[0], wr_ref[...])
    k = _dot(xm_ref[1], wk_ref[...])
    v = _dot(xm_ref[2], wv_ref[...])
    ld = -jax.nn.sigmoid(w0 + _dot(hw_ref[...], w2_ref[...])) * EXP_NEG_HALF
    a = jax.nn.sigmoid(a0 + _dot(ha_ref[...], a2_ref[...]))
    g = _dot(hg_ref[...], g2_ref[...])
    if vres:
        v = v + (vf_ref[...] - v) * jax.nn.sigmoid(v0 + _dot(hv_ref[...], v2_ref[...]))
    kkr = k * k_k
    ss = _split_dot(kkr * kkr, _head_ones(kkr.shape[1], RW_HEAD))
    r_ref[...] = r.astype(r_ref.dtype)
    k_ref[...] = (k * (1.0 + (a - 1.0) * k_a)).astype(k_ref.dtype)
    v_ref[...] = v
    ld_ref[...] = ld
    kk_ref[...] = (kkr / jnp.maximum(jnp.sqrt(ss), 1e-12)).astype(kk_ref.dtype)
    a_ref[...] = a.astype(a_ref.dtype)
    g_ref[...] = g.astype(g_ref.dtype)


def _pad_cols(w, n):
    return jnp.pad(w, ((0, 0), (0, n - w.shape[1])))


def _pad_rows(w, n):
    return jnp.pad(w, ((0, n - w.shape[0]), (0, 0)))


def rwkv_proj(h, nw, p, v_first, seq):
    m, d = h.shape
    tm = min(RW_ROW_TILE, seq)
    tn = min(RW_COL_TILE, d)
    vres = v_first is not None
    lw, la, lg = (-(-p[n].shape[1] // LANE) * LANE for n in ("w1", "a1", "g1"))
    full = lambda shape: pl.BlockSpec(shape, lambda i, j: (0,) * len(shape))
    col = lambda rows: pl.BlockSpec((rows, tn), lambda i, j: (0, j))
    tile = pl.BlockSpec((tm, tn), lambda i, j: (i, j))
    rkv = lambda q: pl.BlockSpec((None, d, tn), lambda i, j: (q, 0, j))
    zeros = jnp.zeros((d,), F32)
    vec = jnp.stack([p["w0"], p["a0"], p["k_k"], p["k_a"], p["v0"] if vres else zeros,
                     zeros, zeros, zeros])
    mu = jnp.pad(p["mu"], ((0, SUBLANE - p["mu"].shape[0]), (0, 0)))
    args = [h, nw.reshape(1, d), mu, p["w_rkv"], p["w_rkv"], p["w_rkv"],
            _pad_cols(p["w1"], lw), _pad_cols(p["a1"], la), _pad_cols(p["g1"], lg),
            _pad_rows(p["w2"], lw), _pad_rows(p["a2"], la), _pad_rows(p["g2"], lg), vec]
    in_specs = [pl.BlockSpec((tm, d), lambda i, j: (i, 0)), full((1, d)), full((SUBLANE, d)),
                rkv(0), rkv(1), rkv(2), full((d, lw)), full((d, la)), full((d, lg)),
                col(lw), col(la), col(lg), col(SUBLANE)]
    scratch = [pltpu.VMEM((6, tm, d), BF16), pltpu.VMEM((tm, lw), BF16),
               pltpu.VMEM((tm, la), BF16), pltpu.VMEM((tm, lg), BF16)]
    if vres:
        lv = -(-p["v1"].shape[1] // LANE) * LANE
        args += [_pad_cols(p["v1"], lv), _pad_rows(p["v2"], lv), v_first]
        in_specs += [full((d, lv)), col(lv), tile]
        scratch.append(pltpu.VMEM((tm, lv), BF16))
    scratch.append(pltpu.VMEM((SUBLANE, d), F32))
    out = jax.ShapeDtypeStruct((m, d), F32)
    half = jax.ShapeDtypeStruct((m, d), BF16)
    kern = functools.partial(_rwkv_proj_kernel, vres=vres, tiles_per_seq=seq // tm)
    return pl.pallas_call(
        kern,
        out_shape=[half, half, out, out, half, half, half],
        grid=(m // tm, d // tn),
        in_specs=in_specs,
        out_specs=[tile] * 7,
        scratch_shapes=scratch,
        compiler_params=_params("arbitrary", "arbitrary"),
        name="rwkv_proj",
    )(*args)


def _rwkv_group(r, ld, k, v, kk, a, z, masks):
    tri, bdm, strict, incl, eye = masks
    C, W = r[0].shape
    G = W // RW_HEAD
    bfl = lambda xs: [x.astype(BF16) for x in xs]

    def each(f, *ls):
        return [f(*xs) for xs in zip(*ls)]

    def bd(xs):
        return [jnp.where(bdm, jnp.concatenate([x.astype(BF16)] * G, axis=0), 0) for x in xs]

    ld = [x * LOG2_E for x in ld]
    cum = each(lambda x: _split_dot_left(tri, x), ld)
    last = [c[C - 1:C, :] for c in cum]
    b = each(jnp.multiply, kk, a)
    einv = [jnp.exp2(-c) for c in cum]
    etail = each(lambda l, c: jnp.exp2(l - c), last, cum)
    at = each(lambda x, c, l: -x * jnp.exp2(c - l), kk, cum, ld)
    rt = each(lambda x, c: x * jnp.exp2(c), r, cum)
    bt = each(jnp.multiply, b, einv)
    kt = each(jnp.multiply, k, einv)
    bh = each(jnp.multiply, b, etail)
    kh = each(jnp.multiply, k, etail)

    vb = bd(v)
    lhs = each(lambda x, y: jnp.concatenate([x, y], axis=0).astype(BF16), at, rt)
    rhs = each(lambda x, y: jnp.concatenate([x, y], axis=0), bd(bt), bd(kt))
    gram = each(_dot_nt, lhs, rhs)
    a_ab = [jnp.where(strict, x[:C, :W], 0.0) for x in gram]
    a_ak = [jnp.where(strict, x[:C, W:], 0.0).astype(BF16) for x in gram]
    a_rb = [jnp.where(incl, x[C:, :W], 0.0).astype(BF16) for x in gram]
    a_rk = [jnp.where(incl, x[C:, W:], 0.0).astype(BF16) for x in gram]

    rows2 = lambda x, y: jnp.concatenate([x, y], axis=0).astype(BF16)

    t = [eye + x for x in a_ab]
    p = each(_dot, bfl(a_ab), bd(a_ab))
    for _ in range(int(math.log2(C)) - 2):
        st = each(_dot, each(rows2, t, p), bd(p))
        t = each(lambda x, s: x + s[:C], t, st)
        p = [s[C:] for s in st]
    tb = each(lambda x, y: (x + _dot(x.astype(BF16), y)).astype(BF16), t, bd(p))

    wa = each(_dot, tb, bd(at))
    st = each(_dot, each(rows2, a_ak, a_rk), vb)
    uv = each(_dot, tb, bd([s[:C] for s in st]))
    zb = bfl(z)
    sz = each(_dot_nt, each(rows2, wa, rt), zb)
    u = each(lambda s, x: s[:C] + x, sz, uv)
    y = each(lambda s, x, ub, w: s[C:] + _dot(x, ub) + w[C:], sz, a_rb, bd(u), st)
    gm = each(lambda x, w, s, q: jnp.where(
        bdm, _dot_tn(jnp.concatenate([x, w], axis=0).astype(BF16),
                     jnp.concatenate([s, q], axis=0).astype(BF16)), 0.0), u, v, bh, kh)
    zn = each(lambda s, l, g_: s * jnp.exp2(l) + g_, z, last, gm)
    return y, zn


def _split_dot_left(w, x):
    hi = x.astype(BF16)
    lo = (x - hi.astype(F32)).astype(BF16)
    return _dot(w, hi) + _dot(w, lo)


def _rwkv_masks(C, W):
    trow = lax.broadcasted_iota(jnp.int32, (C, C), 0)
    tcol = lax.broadcasted_iota(jnp.int32, (C, C), 1)
    tri = (trow >= tcol).astype(BF16)
    brow = lax.broadcasted_iota(jnp.int32, (W, W), 0)
    bcol = lax.broadcasted_iota(jnp.int32, (W, W), 1)
    bdm = (brow // C) == (bcol // RW_HEAD)
    lrow = lax.broadcasted_iota(jnp.int32, (C, W), 0)
    lcol = lax.broadcasted_iota(jnp.int32, (C, W), 1) % RW_HEAD
    eye = jnp.where(lrow == lcol, 1.0, 0.0)
    return tri, bdm, lrow > lcol, lrow >= lcol, eye


def _rwkv_chunk_kernel(r_ref, ld_ref, k_ref, v_ref, kk_ref, a_ref, g_ref, vec_ref, o_ref, z_ref):
    @pl.when(pl.program_id(2) == 0)
    def _():
        z_ref[...] = jnp.zeros_like(z_ref)

    C = r_ref.shape[0]
    W = RW_GROUP * RW_HEAD
    masks = _rwkv_masks(C, W)
    hsum = _head_ones(W, RW_HEAD)
    inv_n = 1.0 / RW_HEAD
    ng = r_ref.shape[1] // W
    sls = [slice(gi * W, (gi + 1) * W) for gi in range(ng)]
    take = lambda ref: [ref[:, sl].astype(F32) for sl in sls]
    r, k, v = take(r_ref), take(k_ref), take(v_ref)
    ys, zs = _rwkv_group(r, take(ld_ref), k, v, take(kk_ref), take(a_ref),
                         [z_ref[gi] for gi in range(ng)], masks)
    for gi in range(ng):
        z_ref[gi] = zs[gi]
    vecs = [vec_ref[:, sl] for sl in sls]
    rk = [r[gi] * k[gi] * vecs[gi][0:1, :] for gi in range(ng)]
    sums = [_dot(jnp.concatenate([ys[gi], rk[gi]], axis=0).astype(BF16), hsum) for gi in range(ng)]
    yc = [ys[gi] - sums[gi][:C] * inv_n for gi in range(ng)]
    var = [_dot((x * x).astype(BF16), hsum) * inv_n for x in yc]
    for gi, sl in enumerate(sls):
        lnx_w, lnx_b = vecs[gi][1:2, :], vecs[gi][2:3, :]
        yn = yc[gi] * lax.rsqrt(var[gi] + RW_GN_EPS) * lnx_w + lnx_b
        bonus = sums[gi][C:] * v[gi]
        o_ref[:, sl] = ((yn + bonus) * g_ref[:, sl].astype(F32)).astype(o_ref.dtype)


def rwkv_scan(r, ld, k, v, kk, a, g, r_k, lnx_w, lnx_b, batch, seq):
    m, d = r.shape
    C = RW_CHUNK
    W = RW_GROUP * RW_HEAD
    assert RW_GROUP * C == W
    wb = min(RW_STEP_GROUPS * W, d)
    nc = seq // C
    zeros = jnp.zeros((d,), F32)
    vec = jnp.stack([r_k.reshape(d), lnx_w, lnx_b] + [zeros] * (SUBLANE - 3))
    spec = pl.BlockSpec((C, wb), lambda b, g, c: (b * nc + c, g))
    return pl.pallas_call(
        _rwkv_chunk_kernel,
        out_shape=jax.ShapeDtypeStruct((m, d), BF16),
        grid=(batch, d // wb, nc),
        in_specs=[spec] * 7 + [pl.BlockSpec((SUBLANE, wb), lambda b, g, c: (0, g))],
        out_specs=spec,
        scratch_shapes=[pltpu.VMEM((wb // W, W, W), F32)],
        compiler_params=_params("parallel", "parallel", "arbitrary"),
        name="rwkv_scan",
    )(r, ld, k, v, kk, a, g, vec)


def _retention_kernel(q_ref, k_ref, v_ref, g_ref, cos_ref, sin_ref, intra_ref, cross_ref,
                      tail_ref, gnw_ref, o_ref, st_ref):
    c = pl.program_id(2)

    @pl.when(c == 0)
    def _():
        st_ref[...] = jnp.zeros_like(st_ref)

    C = q_ref.shape[0]
    half = RET_QK_HEAD // 2
    cos = cos_ref[...]
    sin = sin_ref[...]

    def rope(t):
        t1 = t[:, :half]
        t2 = t[:, half:]
        return jnp.concatenate([t1 * cos - t2 * sin, t1 * sin + t2 * cos], axis=-1)

    hs = range(q_ref.shape[1] // RET_QK_HEAD)
    qk = lambda ref, h: ref[:, h * RET_QK_HEAD:(h + 1) * RET_QK_HEAD].astype(F32)
    vsl = lambda h: slice(h * RET_V_HEAD, (h + 1) * RET_V_HEAD)
    qb = [rope(qk(q_ref, h)).astype(BF16) for h in hs]
    k = [rope(qk(k_ref, h)) * (RET_QK_HEAD ** -0.5) for h in hs]
    vb = [v_ref[:, vsl(h)].astype(BF16) for h in hs]
    cross = [cross_ref[h] for h in hs]
    scores = [(_dot_nt(qb[h], k[h].astype(BF16)) * intra_ref[h]).astype(BF16) for h in hs]
    st = [st_ref[h] for h in hs]
    o = [_dot(scores[h], vb[h]) + _dot(qb[h], st[h].astype(BF16)) * cross[h] for h in hs]
    for h in hs:
        st_ref[h] = st[h] * cross[h][C - 1:C, :] + _dot_tn((k[h] * tail_ref[h]).astype(BF16), vb[h])
    for h in hs:
        mu = jnp.mean(o[h], axis=-1, keepdims=True)
        oc = o[h] - mu
        var = jnp.mean(oc * oc, axis=-1, keepdims=True)
        on = oc * lax.rsqrt(var + GN_EPS) * gnw_ref[:, vsl(h)]
        g = g_ref[:, vsl(h)].astype(F32)
        o_ref[:, vsl(h)] = (g * jax.nn.sigmoid(g) * on).astype(o_ref.dtype)


def retention(proj, gn_w, batch, seq):
    m, n = proj.shape
    heads = n // (2 * RET_QK_HEAD + 2 * RET_V_HEAD)
    C = RET_CHUNK
    nc = seq // C
    half = RET_QK_HEAD // 2
    pos = jnp.arange(seq, dtype=F32)
    inv_freq = 1.0 / (ROPE_BASE ** (jnp.arange(0, RET_QK_HEAD, 2, dtype=F32) / RET_QK_HEAD))
    ang = pos[:, None] * inv_freq[None, :]
    cos, sin = jnp.cos(ang), jnp.sin(ang)
    log_gamma = jnp.log1p(-jnp.exp2(-5.0 - jnp.arange(heads, dtype=F32)))
    cp = jnp.arange(C, dtype=F32)
    rel = cp[:, None] - cp[None, :]
    intra = jnp.where(rel >= 0, jnp.exp(log_gamma[:, None, None] * jnp.maximum(rel, 0.0)), 0.0)
    cross = jnp.exp(log_gamma[:, None] * (cp + 1.0))[:, :, None]
    tail = jnp.exp(log_gamma[:, None] * (C - 1.0 - cp))[:, :, None]
    vdim = heads * RET_V_HEAD
    nh = min(RET_STEP_HEADS, heads)
    nb = heads // nh
    qw, vw = nh * RET_QK_HEAD, nh * RET_V_HEAD
    return pl.pallas_call(
        _retention_kernel,
        out_shape=jax.ShapeDtypeStruct((m, vdim), BF16),
        grid=(batch, nb, nc),
        in_specs=[pl.BlockSpec((C, qw), lambda b, h, c: (b * nc + c, h)),
                  pl.BlockSpec((C, qw), lambda b, h, c: (b * nc + c, nb + h)),
                  pl.BlockSpec((C, vw), lambda b, h, c: (b * nc + c, nb + h)),
                  pl.BlockSpec((C, vw), lambda b, h, c: (b * nc + c, 2 * nb + h)),
                  pl.BlockSpec((C, half), lambda b, h, c: (c, 0)),
                  pl.BlockSpec((C, half), lambda b, h, c: (c, 0)),
                  pl.BlockSpec((nh, C, C), lambda b, h, c: (h, 0, 0)),
                  pl.BlockSpec((nh, C, 1), lambda b, h, c: (h, 0, 0)),
                  pl.BlockSpec((nh, C, 1), lambda b, h, c: (h, 0, 0)),
                  pl.BlockSpec((1, vw), lambda b, h, c: (0, h))],
        out_specs=pl.BlockSpec((C, vw), lambda b, h, c: (b * nc + c, h)),
        scratch_shapes=[pltpu.VMEM((nh, RET_QK_HEAD, RET_V_HEAD), F32)],
        compiler_params=_params("parallel", "parallel", "arbitrary"),
        name="retention",
    )(proj, proj, proj, proj, cos, sin, intra, cross, tail, gn_w.reshape(1, vdim))


def _hgrn_tables(C, wd):
    r = jnp.arange(C)
    gather, bias_q, bias_k, masks = [], [], [], []
    b = C // 2
    while b >= 1:
        ref = (r // (2 * b)) * (2 * b) + b - 1
        gather.append(jax.nn.one_hot(ref, C, dtype=BF16))
        upper = (r & b) != 0
        bias_q.append(jnp.where(upper, 0.0, -jnp.inf))
        bias_k.append(jnp.where(upper, -jnp.inf, 0.0))
        masks.append((r[:, None] // (2 * b)) == (r[None, :] // (2 * b)))
        b //= 2
    masks.append(r[:, None] == r[None, :])
    wide = lambda t: jnp.broadcast_to(jnp.stack(t).astype(F32)[:, :, None], (len(t), C, wd))
    return (jnp.concatenate(gather, axis=0), wide(bias_q), wide(bias_k),
            jnp.stack(masks).astype(F32))


def _hgrn_kernel(q_ref, f_ref, i_ref, g_ref, lb_ref, nw_ref, gat_ref, bq_ref, bk_ref, msk_ref,
                 o_ref, zt_ref):
    @pl.when(pl.program_id(2) == 0)
    def _():
        zt_ref[...] = jnp.zeros_like(zt_ref)

    rows, wd = q_ref.shape
    C = gat_ref.shape[1]
    nl = bq_ref.shape[0]
    heads = [slice(h * HG_HEAD, (h + 1) * HG_HEAD) for h in range(wd // HG_HEAD)]
    chunks = [slice(r0, r0 + C) for r0 in range(0, rows, C)]
    lb = lb_ref[...]
    qr = q_ref[...]
    q = qr * jax.nn.sigmoid(qr)
    sig = jax.nn.sigmoid(f_ref[...])
    lf = jnp.log(lb + (1.0 - lb) * sig)
    k = (1.0 - lb) * (1.0 - sig)
    vb = i_ref[...].astype(BF16)

    trow = lax.broadcasted_iota(jnp.int32, (rows, rows), 0)
    tcol = lax.broadcasted_iota(jnp.int32, (rows, rows), 1)
    tri = ((trow >= tcol) & ((trow // C) == (tcol // C))).astype(BF16)
    cum = _split_dot_left(tri, lf) * LOG2_E
    cb = cum.astype(BF16)

    cref = [_dot(gat_ref[...], cb[c, :]) for c in chunks]
    s = [[_dot_nt(q[c, h].astype(BF16), k[c, h].astype(BF16)) * msk_ref[nl] for h in heads]
         for c in chunks]
    for l in range(nl):
        lv = slice(l * C, (l + 1) * C)
        ql = [(q[c, :] * jnp.exp2(cum[c, :] - cr[lv, :] + bq_ref[l])).astype(BF16)
              for c, cr in zip(chunks, cref)]
        kl = [(k[c, :] * jnp.exp2(cr[lv, :] - cum[c, :] + bk_ref[l])).astype(BF16)
              for c, cr in zip(chunks, cref)]
        p = [[_dot_nt(ql[i][:, h], kl[i][:, h]) for h in heads] for i in range(len(chunks))]
        s = [[s[i][j] + (p[i][j] if l == 0 else p[i][j] * msk_ref[l]) for j in range(len(heads))]
             for i in range(len(chunks))]
    intra = [[_dot(s[i][j].astype(BF16), vb[c, h]) for j, h in enumerate(heads)]
             for i, c in enumerate(chunks)]

    gate = nw_ref[...] * jax.nn.sigmoid(g_ref[...])
    zt = [zt_ref[j] for j in range(len(heads))]
    for i, c in enumerate(chunks):
        cc = cum[c, :]
        last = cc[C - 1:C, :]
        qe = (q[c, :] * jnp.exp2(cc)).astype(BF16)
        kd = (k[c, :] * jnp.exp2(last - cc)).astype(BF16)
        dec = jnp.exp2(last)
        o = [_dot_nt(qe[:, h], zt[j].astype(BF16)) + intra[i][j] for j, h in enumerate(heads)]
        zt = [zt[j] * dec[:, h] + _dot_tn(vb[c, h], kd[:, h]) for j, h in enumerate(heads)]
        on = [x * lax.rsqrt(jnp.mean(x * x, axis=-1, keepdims=True) + RMS_EPS) for x in o]
        o_ref[c, :] = (jnp.concatenate(on, axis=-1) * gate[c, :]).astype(o_ref.dtype)
    for j in range(len(heads)):
        zt_ref[j] = zt[j]


def hgrn_mix(proj, lb, norm_w, batch, seq):
    m, n = proj.shape
    d = n // 4
    wb = min(HG_GROUP * HG_HEAD, d)
    nb = d // wb
    rows = min(HG_STEP_CHUNKS * HG_CHUNK, seq)
    nc = seq // rows
    tables = _hgrn_tables(HG_CHUNK, wb)

    def spec(off):
        return pl.BlockSpec((rows, wb), lambda b, h, c: (b * nc + c, off * nb + h))

    vec = pl.BlockSpec((1, wb), lambda b, h, c: (0, h))
    const = lambda t: pl.BlockSpec(t.shape, lambda b, h, c: (0,) * t.ndim)
    return pl.pallas_call(
        _hgrn_kernel,
        out_shape=jax.ShapeDtypeStruct((m, d), BF16),
        grid=(batch, nb, nc),
        in_specs=[spec(0), spec(1), spec(2), spec(3), vec, vec] + [const(t) for t in tables],
        out_specs=spec(0),
        scratch_shapes=[pltpu.VMEM((wb // HG_HEAD, HG_HEAD, HG_HEAD), F32)],
        compiler_params=_params("parallel", "parallel", "arbitrary"),
        name="hgrn",
    )(proj, proj, proj, proj, lb.reshape(1, d), norm_w.reshape(1, d), *tables)


def kernel(x, norm_mix, norm_ffn, norm_final, ffn_w_up, ffn_conv, ffn_w_down, rw_mu, rw_w_rkv, rw_w0, rw_w1, rw_w2, rw_a0, rw_a1, rw_a2, rw_v0, rw_v1, rw_v2, rw_g1, rw_g2, rw_k_k, rw_k_a, rw_r_k, rw_lnx_w, rw_lnx_b, rw_w_o, ret_w_in, ret_gn_w, ret_w_o, hg_w_in, hg_lb_logits, hg_norm_w, hg_w_o):
    batch, seq, d = x.shape
    depth = norm_mix.shape[0]
    m = batch * seq
    lb_all = jnp.cumsum(jax.nn.softmax(hg_lb_logits.astype(F32), axis=0), axis=0)
    lb_all = lb_all - lb_all[0]
    bf = lambda t: t.astype(BF16)

    h = x.reshape(m, d)
    v_first = None
    for layer in range(depth):
        kind = layer % 3
        j = layer // 3
        if kind == 0:
            p = dict(mu=rw_mu[j], w_rkv=bf(rw_w_rkv[j]), w0=rw_w0[j], w1=bf(rw_w1[j]), w2=bf(rw_w2[j]),
                     a0=rw_a0[j], a1=bf(rw_a1[j]), a2=bf(rw_a2[j]), g1=bf(rw_g1[j]), g2=bf(rw_g2[j]),
                     k_k=rw_k_k[j], k_a=rw_k_a[j])
            if j > 0:
                p.update(v0=rw_v0[j - 1], v1=bf(rw_v1[j - 1]), v2=bf(rw_v2[j - 1]))
            r, k, v, ld, kk, a, g = rwkv_proj(h, norm_mix[layer], p, v_first, seq)
            if v_first is None:
                v_first = v
            mixed = rwkv_scan(r, ld, k, v, kk, a, g, rw_r_k[j], rw_lnx_w[j], rw_lnx_b[j], batch, seq)
            h = matmul_res(mixed, bf(rw_w_o[j]), h)
        elif kind == 1:
            proj = norm_matmul(h, norm_mix[layer], bf(ret_w_in[j]), BF16)
            gated = retention(proj, ret_gn_w[j], batch, seq)
            h = matmul_res(gated, bf(ret_w_o[j]), h)
        else:
            proj = norm_matmul(h, norm_mix[layer], bf(hg_w_in[j]))
            gated = hgrn_mix(proj, lb_all[layer], hg_norm_w[j], batch, seq)
            h = matmul_res(gated, bf(hg_w_o[j]), h)
        final_w = norm_final if layer == depth - 1 else None
        h = ffn_block(h, norm_ffn[layer], *_prep_ffn(ffn_w_up[layer], ffn_conv[layer], ffn_w_down[layer]),
                      seq, final_w)
    return h.reshape(batch, seq, d)
```
